```python
import jax
import jax.numpy as jnp
from jax import lax
import numpy as np


D_MODEL = 1024
BATCH = 2
SEQ = 8192
DEPTH = 2

HEAD_DIM = 64
N_BRANCH = 4
ROPE_THETA = 500000.0
PARTIAL_ROT = HEAD_DIM // 4
Q_BLOCK = 128
NORM_EPS = 1e-6

DSA_HEADS = 4
IDX_HEADS = 8
IDX_DIM = 32
IDX_ROT = IDX_DIM // 4
DSA_TOPK_MAX = 256

SB_HEADS = 4

MOBA_HEADS = 4
MOBA_BLOCK = 256
MOBA_TOPK = 3
MOBA_Q_BLOCK = 64

MLA_HEADS = 4
MLA_Q_RANK = 256
MLA_KV_RANK = 128
MLA_NOPE = 64
MLA_ROPE = 32
MLA_V = 64

BRANCH_WIDTH = 4 * HEAD_DIM
D_FF = 4 * D_MODEL

IN_WIDTHS = (
    DSA_HEADS * HEAD_DIM, DSA_HEADS * HEAD_DIM, DSA_HEADS * HEAD_DIM,
    IDX_HEADS * IDX_DIM, IDX_DIM, IDX_HEADS,
    SB_HEADS * HEAD_DIM, SB_HEADS * HEAD_DIM, SB_HEADS * HEAD_DIM,
    MOBA_HEADS * HEAD_DIM, MOBA_HEADS * HEAD_DIM, MOBA_HEADS * HEAD_DIM,
    MLA_Q_RANK, MLA_KV_RANK, MLA_ROPE,
    N_BRANCH * D_MODEL,
)
N_IN = sum(IN_WIDTHS)

kernel_name = 'hybrid_gated_sparse_sb_moba_mla'


def rmsnorm(x, g):
    xf = x.astype(jnp.float32)
    y = xf * lax.rsqrt(jnp.mean(xf * xf, axis=-1, keepdims=True) + NORM_EPS)
    return (y * g.astype(jnp.float32)).astype(x.dtype)


def rope_tables(n_rot, seq):
    inv_freq = 1.0 / (ROPE_THETA ** (jnp.arange(0, n_rot, 2, dtype=jnp.float32) / n_rot))
    ang = jnp.arange(seq, dtype=jnp.float32)[:, None] * inv_freq[None, :]
    return jnp.cos(ang), jnp.sin(ang)


def apply_rope(x, cos, sin):
    n_half = cos.shape[-1]
    n_rot = 2 * n_half
    x1 = x[..., :n_half]
    x2 = x[..., n_half:n_rot]
    c = cos[:, None, :].astype(x.dtype)
    s = sin[:, None, :].astype(x.dtype)
    return jnp.concatenate([x1 * c - x2 * s, x2 * c + x1 * s, x[..., n_rot:]], axis=-1)


def to_heads(a, n):
    return a.reshape(a.shape[0], a.shape[1], n, a.shape[2] // n)


def split_columns(z):
    offsets = []
    acc = 0
    for w in IN_WIDTHS[:-1]:
        acc += w
        offsets.append(acc)
    return jnp.split(z, offsets, axis=-1)


def sweep_query_blocks(block_fn, n_blocks):
    out = lax.map(block_fn, jnp.arange(n_blocks))
    nb, b, qb, h, d = out.shape
    return out.transpose(1, 0, 2, 3, 4).reshape(b, nb * qb, h, d)


def dsa_attention(q, k, v, qi, ki, wi):
    S = q.shape[1]
    top = min(DSA_TOPK_MAX, S // 4)
    scale = HEAD_DIM ** -0.5
    key_pos = jnp.arange(S)
    gather = jax.vmap(lambda kk, ii: kk[ii])

    def block(i):
        start = i * Q_BLOCK
        t = start + jnp.arange(Q_BLOCK)
        qb = lax.dynamic_slice_in_dim(q, start, Q_BLOCK, axis=1)
        qib = lax.dynamic_slice_in_dim(qi, start, Q_BLOCK, axis=1)
        wib = lax.dynamic_slice_in_dim(wi, start, Q_BLOCK, axis=1).astype(jnp.float32)
        rel = jax.nn.relu(jnp.einsum('bqhd,bsd->bqhs', qib, ki).astype(jnp.float32) * (IDX_DIM ** -0.5))
        score = jnp.einsum('bqhs,bqh->bqs', rel, wib * (IDX_HEADS ** -0.5))
        causal = key_pos[None, :] <= t[:, None]
        score = jnp.where(causal[None], score, -jnp.inf)
        _, sel = lax.top_k(score, top)
        valid = sel <= t[None, :, None]
        kg = gather(k, sel)
        vg = gather(v, sel)
        logits = jnp.einsum('bqhd,bqkhd->bhqk', qb, kg).astype(jnp.float32) * scale
        logits = jnp.where(valid[:, None], logits, -jnp.inf)
        p = jax.nn.softmax(logits, axis=-1).astype(v.dtype)
        return jnp.einsum('bhqk,bqkhd->bqhd', p, vg)

    return sweep_query_blocks(block, S // Q_BLOCK)


def stick_breaking_attention(q, k, v):
    S = q.shape[1]
    scale = HEAD_DIM ** -0.5
    key_pos = jnp.arange(S)

    def block(i):
        start = i * Q_BLOCK
        t = start + jnp.arange(Q_BLOCK)
        qb = lax.dynamic_slice_in_dim(q, start, Q_BLOCK, axis=1)
        z = jnp.einsum('bqhd,bshd->bhqs', qb, k).astype(jnp.float32) * scale
        past = key_pos[None, :] < t[:, None]
        log_fail = jnp.where(past, jax.nn.log_sigmoid(-z), 0.0)
        later = lax.cumsum(log_fail, axis=3, reverse=True) - log_fail
        a = jnp.where(past, jnp.exp(jax.nn.log_sigmoid(z) + later), 0.0)
        return jnp.einsum('bhqs,bshd->bqhd', a.astype(v.dtype), v)

    return sweep_query_blocks(block, S // Q_BLOCK)


def moba_attention(q, k, v):
    B, S, H, dh = q.shape
    nb = -(-S // MOBA_BLOCK)
    s_pad = nb * MOBA_BLOCK
    n_sel = min(MOBA_TOPK, nb)
    scale = dh ** -0.5
    pad = ((0, 0), (0, s_pad - S), (0, 0), (0, 0))
    kp = jnp.pad(k, pad)
    vp = jnp.pad(v, pad)
    kblk = kp.reshape(B, nb, MOBA_BLOCK, H, dh).transpose(0, 3, 1, 2, 4)
    vblk = vp.reshape(B, nb, MOBA_BLOCK, H, dh).transpose(0, 3, 1, 2, 4)
    kmean = jnp.mean(kblk.astype(jnp.float32), axis=3).astype(k.dtype)
    gather = jax.vmap(jax.vmap(lambda kk, ii: kk[ii]))
    blk_ids = jnp.arange(nb)

    def block(i):
        start = i * MOBA_Q_BLOCK
        t = start + jnp.arange(MOBA_Q_BLOCK)
        own = start // MOBA_BLOCK
        qb = lax.dynamic_slice_in_dim(q, start, MOBA_Q_BLOCK, axis=1)
        gate = jnp.einsum('bqhd,bhnd->bhqn', qb, kmean).astype(jnp.float32)
        gate = jnp.where(blk_ids < own, gate, -jnp.inf)
        _, sel = lax.top_k(gate, n_sel)
        valid = sel < own
        kg = gather(kblk, sel)
        vg = gather(vblk, sel)
        l_sel = jnp.einsum('bqhd,bhqnld->bhqnl', qb, kg).astype(jnp.float32) * scale
        l_sel = jnp.where(valid[..., None], l_sel, -jnp.inf).reshape(B, H, MOBA_Q_BLOCK, n_sel * MOBA_BLOCK)
        k_own = lax.dynamic_slice_in_dim(kp, own * MOBA_BLOCK, MOBA_BLOCK, axis=1)
        v_own = lax.dynamic_slice_in_dim(vp, own * MOBA_BLOCK, MOBA_BLOCK, axis=1)
        l_own = jnp.einsum('bqhd,bshd->bhqs', qb, k_own).astype(jnp.float32) * scale
        own_pos = own * MOBA_BLOCK + jnp.arange(MOBA_BLOCK)
        l_own = jnp.where(own_pos[None, :] <= t[:, None], l_own, -jnp.inf)
        p = jax.nn.softmax(jnp.concatenate([l_sel, l_own], axis=-1), axis=-1).astype(v.dtype)
        p_sel = p[..., :n_sel * MOBA_BLOCK].reshape(B, H, MOBA_Q_BLOCK, n_sel, MOBA_BLOCK)
        p_own = p[..., n_sel * MOBA_BLOCK:]
        return (jnp.einsum('bhqnl,bhqnld->bqhd', p_sel, vg)
                + jnp.einsum('bhqs,bshd->bqhd', p_own, v_own))

    return sweep_query_blocks(block, S // MOBA_Q_BLOCK)


def mla_attention(q_nope, q_rope, k_nope, k_rope, v):
    S = q_nope.shape[1]
    scale = (MLA_NOPE + MLA_ROPE) ** -0.5
    key_pos = jnp.arange(S)

    def block(i):
        start = i * Q_BLOCK
        t = start + jnp.arange(Q_BLOCK)
        qn = lax.dynamic_slice_in_dim(q_nope, start, Q_BLOCK, axis=1)
        qr = lax.dynamic_slice_in_dim(q_rope, start, Q_BLOCK, axis=1)
        logits = (jnp.einsum('bqhd,bshd->bhqs', qn, k_nope)
                  + jnp.einsum('bqhr,bsr->bhqs', qr, k_rope)).astype(jnp.float32) * scale
        logits = jnp.where(key_pos[None, :] <= t[:, None], logits, -jnp.inf)
        p = jax.nn.softmax(logits, axis=-1).astype(v.dtype)
        return jnp.einsum('bhqs,bshd->bqhd', p, v)

    return sweep_query_blocks(block, S // Q_BLOCK)


def setup_inputs(seed: int = 0) -> dict:
    key = jax.random.key(seed)
    ks = jax.random.split(key, 14)
    f32 = jnp.float32

    def nrm(k, shape, scale):
        return jax.random.normal(k, shape, f32) * scale

    return {
        'x': nrm(ks[0], (BATCH, SEQ, D_MODEL), 1.0),
        'norm_mix': 1.0 + nrm(ks[1], (DEPTH, D_MODEL), 0.05),
        'w_in': nrm(ks[2], (DEPTH, D_MODEL, N_IN), D_MODEL ** -0.5),
        'b_gate': nrm(ks[3], (DEPTH, N_BRANCH, D_MODEL), 0.1),
        'mla_q_norm': 1.0 + nrm(ks[4], (DEPTH, MLA_Q_RANK), 0.05),
        'mla_kv_norm': 1.0 + nrm(ks[5], (DEPTH, MLA_KV_RANK), 0.05),
        'w_uq': nrm(ks[6], (DEPTH, MLA_Q_RANK, MLA_HEADS * (MLA_NOPE + MLA_ROPE)), MLA_Q_RANK ** -0.5),
        'w_ukv': nrm(ks[7], (DEPTH, MLA_KV_RANK, MLA_HEADS * (MLA_NOPE + MLA_V)), MLA_KV_RANK ** -0.5),
        'w_branch': nrm(ks[8], (DEPTH, N_BRANCH, BRANCH_WIDTH, D_MODEL), BRANCH_WIDTH ** -0.5),
        'w_out': nrm(ks[9], (DEPTH, D_MODEL, D_MODEL), D_MODEL ** -0.5),
        'norm_mlp': 1.0 + nrm(ks[10], (DEPTH, D_MODEL), 0.05),
        'w_up': nrm(ks[11], (DEPTH, D_MODEL, D_FF), D_MODEL ** -0.5),
        'w_down': nrm(ks[12], (DEPTH, D_FF, D_MODEL), D_FF ** -0.5),
        'norm_final': 1.0 + nrm(ks[13], (D_MODEL,), 0.05),
    }


def reference(x, norm_mix, w_in, b_gate, mla_q_norm, mla_kv_norm, w_uq, w_ukv,
              w_branch, w_out, norm_mlp, w_up, w_down, norm_final):
    B, S, _ = x.shape
    cos_p, sin_p = rope_tables(PARTIAL_ROT, S)
    cos_i, sin_i = rope_tables(IDX_ROT, S)
    cos_m, sin_m = rope_tables(MLA_ROPE, S)

    for l in range(DEPTH):
        h = rmsnorm(x, norm_mix[l])
        z = h @ w_in[l]
        (dq, dk, dv, iq, ik, iw, sq, sk, sv, mq, mk, mv,
         cq, ckv, kr, gate_logits) = split_columns(z)

        o_a = dsa_attention(
            apply_rope(to_heads(dq, DSA_HEADS), cos_p, sin_p),
            apply_rope(to_heads(dk, DSA_HEADS), cos_p, sin_p),
            to_heads(dv, DSA_HEADS),
            apply_rope(to_heads(iq, IDX_HEADS), cos_i, sin_i),
            apply_rope(ik[:, :, None, :], cos_i, sin_i)[:, :, 0],
            iw)

        o_b = stick_breaking_attention(to_heads(sq, SB_HEADS), to_heads(sk, SB_HEADS),
                                       to_heads(sv, SB_HEADS))

        o_c = moba_attention(apply_rope(to_heads(mq, MOBA_HEADS), cos_p, sin_p),
                             apply_rope(to_heads(mk, MOBA_HEADS), cos_p, sin_p),
                             to_heads(mv, MOBA_HEADS))

        qu = to_heads(rmsnorm(cq, mla_q_norm[l]) @ w_uq[l], MLA_HEADS)
        kvu = to_heads(rmsnorm(ckv, mla_kv_norm[l]) @ w_ukv[l], MLA_HEADS)
        o_d = mla_attention(qu[..., :MLA_NOPE],
                            apply_rope(qu[..., MLA_NOPE:], cos_m, sin_m),
                            kvu[..., :MLA_NOPE],
                            apply_rope(kr[:, :, None, :], cos_m, sin_m)[:, :, 0],
                            kvu[..., MLA_NOPE:])

        branches = jnp.stack([o_a.reshape(B, S, -1), o_b.reshape(B, S, -1),
                              o_c.reshape(B, S, -1), o_d.reshape(B, S, -1)], axis=2)
        gates = jax.nn.sigmoid(gate_logits.reshape(B, S, N_BRANCH, D_MODEL) + b_gate[l])
        proj = jnp.einsum('bsnc,ncd->bsnd', branches, w_branch[l])
        merged = jnp.sum(gates * proj, axis=2)
        x = x + merged @ w_out[l]

        h2 = rmsnorm(x, norm_mlp[l])
        x = x + jnp.square(jax.nn.relu(h2 @ w_up[l])) @ w_down[l]

    return rmsnorm(x, norm_final)
```

```python
import functools

import jax
import jax.numpy as jnp
from jax import lax
from jax.experimental import pallas as pl
from jax.experimental.pallas import tpu as pltpu

F32 = jnp.float32
BF16 = jnp.bfloat16
I32 = jnp.int32

D_MODEL = 1024
HEAD_DIM = 64
N_HEADS = 4
N_BRANCH = 4
ROPE_THETA = 500000.0
PARTIAL_ROT = HEAD_DIM // 4
NORM_EPS = 1e-6
IDX_HEADS = 8
IDX_DIM = 32
IDX_ROT = IDX_DIM // 4
DSA_TOPK_MAX = 256
MOBA_BLOCK = 256
MOBA_TOPK = 3
MLA_Q_RANK = 256
MLA_KV_RANK = 128
MLA_NOPE = 64
MLA_ROPE = 32
D_FF = 4 * D_MODEL
IN_WIDTHS = (256, 256, 256, 256, 32, 8, 256, 256, 256, 256, 256, 256, 256, 128, 32, 4096)

LANES = 128
HEAD_SLOT = LANES
QKV_W = N_HEADS * HEAD_DIM
VPAD_W = N_HEADS * HEAD_SLOT
ONES_LANE = HEAD_DIM
VMEM_LIMIT = 56 * 1024 * 1024

PROJ_ROWS = 512
ATT_T = 256
NEG = -1e30
INT_MIN = -(2 ** 31)

_NT = (((1,), (1,)), ((), ()))


def _dot(a, b):
    return jnp.dot(a, b, preferred_element_type=F32)


def _dot_nt(a, b):
    return lax.dot_general(a, b, _NT, preferred_element_type=F32)


def _rms(x, g):
    return x * lax.rsqrt(jnp.mean(x * x, axis=-1, keepdims=True) + NORM_EPS) * g


def _rope(x, c, s1, s2, shift):
    w = x.shape[-1]
    return x * c + pltpu.roll(x, w - shift, 1) * s1 + pltpu.roll(x, shift, 1) * s2


def _resident(shape, index_map):
    return pl.BlockSpec(shape, index_map, pipeline_mode=pl.Buffered(1))


def _params(*sem):
    return pltpu.CompilerParams(dimension_semantics=sem, vmem_limit_bytes=VMEM_LIMIT)


_G_A = 0
_G_I = 1024
_G_B = 1664
_G_C = 2688
_G_D = 3712
_G_END = 4608


def _proj_kernel(x_ref, g_ref, w_ref, gq_ref, gkv_ref, wuq_ref, wukv_ref,
                 cp_ref, s1p_ref, s2p_ref, ci_ref, s1i_ref, s2i_ref, cm_ref, s1m_ref, s2m_ref,
                 qa_ref, ka_ref, va_ref, qi_ref, ki_ref, iw_ref,
                 qb_ref, kb_ref, vb_ref, qc_ref, kc_ref, vc_ref, kmean_ref,
                 qd_ref, kd_ref, vd_ref):
    hb = _rms(x_ref[...], g_ref[...]).astype(BF16)
    tm = hb.shape[0]
    cp, s1p, s2p = cp_ref[...], s1p_ref[...], s2p_ref[...]
    ones_lane = (lax.broadcasted_iota(I32, (tm, VPAD_W), 1) % HEAD_SLOT) == ONES_LANE
    qscale = HEAD_DIM ** -0.5

    def group(c0, width):
        return _dot(hb, w_ref[:, c0:c0 + width])

    z = group(_G_A, 1024)
    qa_ref[...] = (_rope(z[:, 0:256], cp, s1p, s2p, PARTIAL_ROT // 2) * qscale).astype(BF16)
    ka_ref[...] = _rope(z[:, 256:512], cp, s1p, s2p, PARTIAL_ROT // 2).astype(BF16)
    va_ref[...] = jnp.where(ones_lane, 1.0, z[:, 512:1024]).astype(BF16)

    z = group(_G_I, 640)
    ci, s1i, s2i = ci_ref[...], s1i_ref[...], s2i_ref[...]
    qi_ref[...] = _rope(z[:, 0:256], ci, s1i, s2i, IDX_ROT // 2).astype(BF16)
    ki_ref[...] = _rope(z[:, 256:512], ci, s1i, s2i, IDX_ROT // 2).astype(BF16)
    iw_ref[...] = z[:, 512:640] * ((IDX_DIM ** -0.5) * (IDX_HEADS ** -0.5))

    z = group(_G_B, 1024)
    qb_ref[...] = (z[:, 0:256] * qscale).astype(BF16)
    kb_ref[...] = z[:, 256:512].astype(BF16)
    vb_ref[...] = z[:, 512:1024].astype(BF16)

    z = group(_G_C, 1024)
    qc_ref[...] = (_rope(z[:, 0:256], cp, s1p, s2p, PARTIAL_ROT // 2) * qscale).astype(BF16)
    kc = _rope(z[:, 256:512], cp, s1p, s2p, PARTIAL_ROT // 2)
    kc_ref[...] = kc.astype(BF16)
    vc_ref[...] = jnp.where(ones_lane, 1.0, z[:, 512:1024]).astype(BF16)
    kmean_ref[0] = jnp.sum(kc.reshape(tm // MOBA_BLOCK, MOBA_BLOCK, QKV_W), axis=1) * (1.0 / MOBA_BLOCK)

    z = group(_G_D, 896)
    cm, s1m, s2m = cm_ref[...], s1m_ref[...], s2m_ref[...]
    cq = _rms(z[:, 0:256], gq_ref[...]).astype(BF16)
    ckv = _rms(z[:, 256:384], gkv_ref[...]).astype(BF16)
    qu = _dot(cq, wuq_ref[...])
    mla_scale = (MLA_NOPE + MLA_ROPE) ** -0.5
    qd_ref[...] = (_rope(qu, cm, s1m, s2m, MLA_ROPE // 2) * mla_scale).astype(BF16)
    kvu = _dot(ckv, wukv_ref[...])
    kd_ref[...] = (kvu[:, 0:512] + _rope(z[:, 384:896], cm, s1m, s2m, MLA_ROPE // 2)).astype(BF16)
    vd_ref[...] = jnp.where(ones_lane, 1.0, kvu[:, 512:1024]).astype(BF16)


def _proj_call(x2, g, w1, gq, gkv, wuq, wukv, tables, seq):
    n = x2.shape[0]
    tm = PROJ_ROWS
    nt = n // tm
    tps = seq // tm
    row = lambda w: pl.BlockSpec((tm, w), lambda i: (i, 0))
    tab = lambda w: pl.BlockSpec((tm, w), lambda i: (i % tps, 0))
    const = lambda a: _resident(a.shape, lambda i: (0,) * a.ndim)
    in_specs = [row(D_MODEL), const(g), const(w1), const(gq), const(gkv), const(wuq), const(wukv),
                tab(256), tab(256), tab(256), tab(256), tab(256), tab(256), tab(512), tab(512), tab(512)]
    kb = tm // MOBA_BLOCK
    out_shapes = []
    out_specs = []

    def add(width, dtype):
        out_shapes.append(jax.ShapeDtypeStruct((n, width), dtype))
        out_specs.append(row(width))

    add(256, BF16); add(256, BF16); add(512, BF16)
    add(256, BF16); add(256, BF16); add(128, F32)
    add(256, BF16); add(256, BF16); add(512, BF16)
    add(256, BF16); add(256, BF16); add(512, BF16)
    out_shapes.append(jax.ShapeDtypeStruct((nt, kb, QKV_W), F32))
    out_specs.append(pl.BlockSpec((1, kb, QKV_W), lambda i: (i, 0, 0)))
    add(512, BF16); add(512, BF16); add(512, BF16)
    return pl.pallas_call(
        _proj_kernel,
        grid=(nt,),
        in_specs=in_specs,
        out_specs=out_specs,
        out_shape=out_shapes,
        compiler_params=_params("arbitrary"),
        name="proj",
    )(x2, g, w1, gq, gkv, wuq, wukv, *tables)


def _stack_heads(q, n_heads, head_dim):
    lane_head = lax.broadcasted_iota(I32, q.shape, 1) // head_dim
    zero = jnp.zeros_like(q)
    return jnp.concatenate([jnp.where(lane_head == h, q, zero) for h in range(n_heads)], axis=0)


def _pv(p, vblk, t):
    return jnp.concatenate(
        [_dot(p[h * t:(h + 1) * t], vblk[:, h * HEAD_SLOT:(h + 1) * HEAD_SLOT]) for h in range(N_HEADS)], axis=0)


def _softmax_step(s, vblk, m_ref, acc_ref, t):
    m_old = m_ref[...]
    m_new = jnp.maximum(m_old, jnp.max(s, axis=1, keepdims=True))
    alpha = jnp.exp(m_old - m_new)
    p = jnp.exp(s - m_new).astype(BF16)
    acc_ref[...] = alpha * acc_ref[...] + _pv(p, vblk, t)
    m_ref[...] = m_new


def _softmax_finish(o_ref, acc_ref, t):
    for h in range(N_HEADS):
        a = acc_ref[h * t:(h + 1) * t, :]
        o_ref[0, :, h * HEAD_SLOT:(h + 1) * HEAD_SLOT] = (a / a[:, ONES_LANE:ONES_LANE + 1]).astype(BF16)


def _causal_rows(t):
    r = lax.broadcasted_iota(I32, (N_HEADS * t, t), 0) % t
    c = lax.broadcasted_iota(I32, (N_HEADS * t, t), 1)
    return c <= r


def _kblock(ref, j, t):
    return ref[0, pl.ds(pl.multiple_of(j * t, t), t), :]


def _att_specs(t, seq, q_widths, kv_widths):
    qs = [pl.BlockSpec((1, t, w), lambda b, i: (b, i, 0)) for w in q_widths]
    kvs = [_resident((1, seq, w), lambda b, i: (b, 0, 0)) for w in kv_widths]
    return qs + kvs


def _att_out(bsz, seq, t):
    return (jax.ShapeDtypeStruct((bsz, seq, VPAD_W), BF16),
            pl.BlockSpec((1, t, VPAD_W), lambda b, i: (b, i, 0)))


def _order_key(x):
    bits = pltpu.bitcast(x + 0.0, I32)
    return bits ^ ((bits >> 31) & 0x7FFFFFFF)


def _dsa_kernel(qa_ref, qi_ref, iw_ref, ka_ref, va_ref, ki_ref, o_ref, keys_ref, m_ref, acc_ref, *, t, topk):
    i = pl.program_id(1)
    nblk = i + 1

    qis = _stack_heads(qi_ref[0], IDX_HEADS, IDX_DIM)
    iw = iw_ref[0]
    w_cols = [iw[:, h:h + 1] for h in range(IDX_HEADS)]
    row_ids = lax.broadcasted_iota(I32, (t, t), 0)
    col_ids = lax.broadcasted_iota(I32, (t, t), 1)

    def score_tile(j, carry):
        r = _dot_nt(qis, _kblock(ki_ref, j, t))
        sc = w_cols[0] * jnp.maximum(r[0:t], 0.0)
        for h in range(1, IDX_HEADS):
            sc = sc + w_cols[h] * jnp.maximum(r[h * t:(h + 1) * t], 0.0)
        key = _order_key(sc)
        key = jnp.where((j < i) | (col_ids <= row_ids), key, INT_MIN)
        keys_ref[:, pl.ds(pl.multiple_of(j * t, t), t)] = key
        return carry

    lax.fori_loop(0, nblk, score_tile, 0)

    def count(pred):
        def body(c, acc):
            hit = pred(keys_ref[:, pl.ds(pl.multiple_of(c * t, t), t)]).astype(I32)
            part = hit[:, 0:LANES]
            for g in range(1, t // LANES):
                part = part + hit[:, g * LANES:(g + 1) * LANES]
            return acc + part
        acc = lax.fori_loop(0, nblk, body, jnp.zeros((t, LANES), I32))
        return jnp.sum(acc, axis=1, keepdims=True)

    c_nonneg = count(lambda k: k >= 0)
    tau0 = jnp.where(c_nonneg >= topk, 0, INT_MIN).astype(I32)

    def bit_step(b, tau):
        cand = tau | lax.shift_left(jnp.int32(1), 30 - b)
        c = count(lambda k: k >= cand)
        return jnp.where(c >= topk, cand, tau)

    tau = lax.fori_loop(0, 31, bit_step, tau0)
    tau = jnp.maximum(tau, INT_MIN + 1)

    c_ge = count(lambda k: k >= tau)

    @pl.when(jnp.max(c_ge) > topk)
    def _():
        c_gt = count(lambda k: k > tau)
        allowed = (topk - c_gt).astype(F32)
        prefix_incl = (row_ids <= col_ids).astype(BF16)

        def fix(c, seen):
            sl = pl.ds(pl.multiple_of(c * t, t), t)
            k = keys_ref[:, sl]
            eq = k == tau
            rank = seen + _dot(eq.astype(BF16), prefix_incl)
            keys_ref[:, sl] = jnp.where(eq & (rank > allowed), tau - 1, k)
            return seen + jnp.sum(eq.astype(F32), axis=1, keepdims=True)

        lax.fori_loop(0, nblk, fix, jnp.zeros((t, 1), F32))

    qs = _stack_heads(qa_ref[0], N_HEADS, HEAD_DIM)
    m_ref[...] = jnp.full(m_ref.shape, NEG, F32)
    acc_ref[...] = jnp.zeros(acc_ref.shape, F32)

    def attend(j, carry):
        s = _dot_nt(qs, _kblock(ka_ref, j, t))
        sel = keys_ref[:, pl.ds(pl.multiple_of(j * t, t), t)] >= tau
        s = jnp.concatenate([jnp.where(sel, s[h * t:(h + 1) * t], NEG) for h in range(N_HEADS)], axis=0)
        _softmax_step(s, _kblock(va_ref, j, t), m_ref, acc_ref, t)
        return carry

    lax.fori_loop(0, nblk, attend, 0)
    _softmax_finish(o_ref, acc_ref, t)


def _dsa_call(qa, qi, iw, ka, va, ki):
    bsz, seq, _ = qa.shape
    t = ATT_T
    topk = min(DSA_TOPK_MAX, seq // 4)
    out_shape, out_spec = _att_out(bsz, seq, t)
    return pl.pallas_call(
        functools.partial(_dsa_kernel, t=t, topk=topk),
        grid=(bsz, seq // t),
        in_specs=_att_specs(t, seq, (QKV_W, QKV_W, LANES), (QKV_W, VPAD_W, QKV_W)),
        out_specs=out_spec,
        out_shape=out_shape,
        scratch_shapes=[pltpu.VMEM((t, seq), I32),
                        pltpu.VMEM((N_HEADS * t, 1), F32),
                        pltpu.VMEM((N_HEADS * t, HEAD_SLOT), F32)],
        compiler_params=_params("arbitrary", "arbitrary"),
        name="dsa",
    )(qa, qi, iw, ka, va, ki)


def _sb_kernel(q_ref, k_ref, v_ref, o_ref, carry_ref, acc_ref, *, t):
    i = pl.program_id(1)
    qs = _stack_heads(q_ref[0], N_HEADS, HEAD_DIM)
    rows = N_HEADS * t
    suffix = (lax.broadcasted_iota(I32, (t, t), 0) > lax.broadcasted_iota(I32, (t, t), 1)).astype(BF16)
    carry_ref[...] = jnp.zeros(carry_ref.shape, F32)
    acc_ref[...] = jnp.zeros(acc_ref.shape, F32)

    def step(j, past):
        z = _dot_nt(qs, _kblock(k_ref, j, t))
        lf = -(jnp.maximum(z, 0.0) + jnp.log1p(jnp.exp(-jnp.abs(z))))
        if past is not None:
            lf = jnp.where(past, lf, 0.0)
        hi = lf.astype(BF16)
        lo = (lf - hi.astype(F32)).astype(BF16)
        later = _dot(jnp.concatenate([hi, lo], axis=0), suffix)
        later = later[0:rows] + later[rows:2 * rows] + carry_ref[...]
        a = jnp.exp(z + lf + later)
        if past is not None:
            a = jnp.where(past, a, 0.0)
        acc_ref[...] += _pv(a.astype(BF16), _kblock(v_ref, j, t), t)
        carry_ref[...] += jnp.sum(lf, axis=1, keepdims=True)

    r = lax.broadcasted_iota(I32, (rows, t), 0) % t
    c = lax.broadcasted_iota(I32, (rows, t), 1)
    step(i, c < r)

    def body(n, carry):
        step(i - 1 - n, None)
        return carry

    lax.fori_loop(0, i, body, 0)
    for h in range(N_HEADS):
        o_ref[0, :, h * HEAD_SLOT:(h + 1) * HEAD_SLOT] = acc_ref[h * t:(h + 1) * t, :].astype(BF16)


def _sb_call(q, k, v):
    bsz, seq, _ = q.shape
    t = ATT_T
    out_shape, out_spec = _att_out(bsz, seq, t)
    return pl.pallas_call(
        functools.partial(_sb_kernel, t=t),
        grid=(bsz, seq // t),
        in_specs=_att_specs(t, seq, (QKV_W,), (QKV_W, VPAD_W)),
        out_specs=out_spec,
        out_shape=out_shape,
        scratch_shapes=[pltpu.VMEM((N_HEADS * t, 1), F32),
                        pltpu.VMEM((N_HEADS * t, HEAD_SLOT), F32)],
        compiler_params=_params("arbitrary", "arbitrary"),
        name="stickbreak",
    )(q, k, v)


def _moba_kernel(q_ref, kmean_ref, k_ref, v_ref, o_ref, m_ref, acc_ref, *, t, n_sel):
    i = pl.program_id(1)
    qs = _stack_heads(q_ref[0], N_HEADS, HEAD_DIM)
    rows = N_HEADS * t
    gate = _dot_nt(qs, kmean_ref[0])
    blk = lax.broadcasted_iota(I32, (rows, LANES), 1)
    ninf = -jnp.inf
    g = jnp.where(blk < i, gate, ninf)
    chosen = jnp.zeros((rows, LANES), F32)
    for _ in range(n_sel):
        mx = jnp.max(g, axis=1, keepdims=True)
        first = jnp.min(jnp.where(g == mx, blk, LANES), axis=1, keepdims=True)
        pick = (blk == first) & (mx > ninf)
        chosen = jnp.where(pick, 1.0, chosen)
        g = jnp.where(pick, ninf, g)

    m_ref[...] = jnp.full(m_ref.shape, NEG, F32)
    acc_ref[...] = jnp.zeros(acc_ref.shape, F32)

    def past_block(j, carry):
        row_on = jnp.max(jnp.where(blk == j, chosen, 0.0), axis=1, keepdims=True) > 0.0
        s = _dot_nt(qs, _kblock(k_ref, j, t))
        s = jnp.where(row_on, s, NEG)
        _softmax_step(s, _kblock(v_ref, j, t), m_ref, acc_ref, t)
        return carry

    lax.fori_loop(0, i, past_block, 0)
    s = jnp.where(_causal_rows(t), _dot_nt(qs, _kblock(k_ref, i, t)), NEG)
    _softmax_step(s, _kblock(v_ref, i, t), m_ref, acc_ref, t)
    _softmax_finish(o_ref, acc_ref, t)


def _moba_call(q, kmean, k, v):
    bsz, seq, _ = q.shape
    t = ATT_T
    assert t == MOBA_BLOCK
    n_sel = min(MOBA_TOPK, seq // MOBA_BLOCK)
    out_shape, out_spec = _att_out(bsz, seq, t)
    in_specs = ([pl.BlockSpec((1, t, QKV_W), lambda b, i: (b, i, 0)),
                 _resident((1, LANES, QKV_W), lambda b, i: (b, 0, 0))]
                + [_resident((1, seq, w), lambda b, i: (b, 0, 0)) for w in (QKV_W, VPAD_W)])
    return pl.pallas_call(
        functools.partial(_moba_kernel, t=t, n_sel=n_sel),
        grid=(bsz, seq // t),
        in_specs=in_specs,
        out_specs=out_spec,
        out_shape=out_shape,
        scratch_shapes=[pltpu.VMEM((N_HEADS * t, 1), F32),
                        pltpu.VMEM((N_HEADS * t, HEAD_SLOT), F32)],
        compiler_params=_params("arbitrary", "arbitrary"),
        name="moba",
    )(q, kmean, k, v)


def _mla_kernel(q_ref, k_ref, v_ref, o_ref, m_ref, acc_ref, *, t):
    i = pl.program_id(1)
    q = q_ref[0]
    q_heads = [q[:, h * HEAD_SLOT:(h + 1) * HEAD_SLOT] for h in range(N_HEADS)]
    m_ref[...] = jnp.full(m_ref.shape, NEG, F32)
    acc_ref[...] = jnp.zeros(acc_ref.shape, F32)

    def logits(j):
        kblk = _kblock(k_ref, j, t)
        return jnp.concatenate(
            [_dot_nt(q_heads[h], kblk[:, h * HEAD_SLOT:(h + 1) * HEAD_SLOT]) for h in range(N_HEADS)], axis=0)

    def past_block(j, carry):
        _softmax_step(logits(j), _kblock(v_ref, j, t), m_ref, acc_ref, t)
        return carry

    lax.fori_loop(0, i, past_block, 0)
    s = jnp.where(_causal_rows(t), logits(i), NEG)
    _softmax_step(s, _kblock(v_ref, i, t), m_ref, acc_ref, t)
    _softmax_finish(o_ref, acc_ref, t)


def _mla_call(q, k, v):
    bsz, seq, _ = q.shape
    t = ATT_T
    out_shape, out_spec = _att_out(bsz, seq, t)
    return pl.pallas_call(
        functools.partial(_mla_kernel, t=t),
        grid=(bsz, seq // t),
        in_specs=_att_specs(t, seq, (VPAD_W,), (VPAD_W, VPAD_W)),
        out_specs=out_spec,
        out_shape=out_shape,
        scratch_shapes=[pltpu.VMEM((N_HEADS * t, 1), F32),
                        pltpu.VMEM((N_HEADS * t, HEAD_SLOT), F32)],
        compiler_params=_params("arbitrary", "arbitrary"),
        name="mla",
    )(q, k, v)


def _merge_kernel(x_ref, g_ref, wg_ref, bg_ref, oa_ref, ob_ref, oc_ref, od_ref, wb_ref, wout_ref, out_ref):
    x = x_ref[...]
    hb = _rms(x, g_ref[...]).astype(BF16)
    merged = None
    for n, o_ref in enumerate((oa_ref, ob_ref, oc_ref, od_ref)):
        logit = _dot(hb, wg_ref[:, n * D_MODEL:(n + 1) * D_MODEL]) + bg_ref[n:n + 1, :]
        gate = 1.0 / (1.0 + jnp.exp(-logit))
        term = gate * _dot(o_ref[...], wb_ref[n])
        merged = term if merged is None else merged + term
    out_ref[...] = x + _dot(merged.astype(BF16), wout_ref[...])


def _merge_call(x2, g, wg, bg, oa, ob, oc, od, wb, wout):
    n = x2.shape[0]
    tm = PROJ_ROWS
    row = lambda w: pl.BlockSpec((tm, w), lambda i: (i, 0))
    const = lambda a: _resident(a.shape, lambda i: (0,) * a.ndim)
    return pl.pallas_call(
        _merge_kernel,
        grid=(n // tm,),
        in_specs=[row(D_MODEL), const(g), const(wg), const(bg), row(VPAD_W), row(VPAD_W), row(VPAD_W),
                  row(VPAD_W), const(wb), const(wout)],
        out_specs=row(D_MODEL),
        out_shape=jax.ShapeDtypeStruct((n, D_MODEL), F32),
        compiler_params=_params("arbitrary"),
        name="merge",
    )(x2, g, wg, bg, oa, ob, oc, od, wb, wout)


def _mlp_kernel(x_ref, g_ref, wup_ref, wdown_ref, gf_ref, out_ref, *, final_norm):
    x = x_ref[...]
    hb = _rms(x, g_ref[...]).astype(BF16)
    y = x
    for c in range(D_FF // D_MODEL):
        u = jnp.maximum(_dot(hb, wup_ref[:, c * D_MODEL:(c + 1) * D_MODEL]), 0.0)
        y = y + _dot((u * u).astype(BF16), wdown_ref[c * D_MODEL:(c + 1) * D_MODEL, :])
    if final_norm:
        y = _rms(y, gf_ref[...])
    out_ref[...] = y


def _mlp_call(x2, g, wup, wdown, gf, final_norm):
    n = x2.shape[0]
    tm = PROJ_ROWS
    row = lambda w: pl.BlockSpec((tm, w), lambda i: (i, 0))
    const = lambda a: _resident(a.shape, lambda i: (0,) * a.ndim)
    return pl.pallas_call(
        functools.partial(_mlp_kernel, final_norm=final_norm),
        grid=(n // tm,),
        in_specs=[row(D_MODEL), const(g), const(wup), const(wdown), const(gf)],
        out_specs=row(D_MODEL),
        out_shape=jax.ShapeDtypeStruct((n, D_MODEL), F32),
        compiler_params=_params("arbitrary"),
        name="mlp",
    )(x2, g, wup, wdown, gf)


def _pad_last(w, n_groups, width, slot):
    lead = w.shape[:-1]
    w = w.reshape(*lead, n_groups, width)
    w = jnp.pad(w, [(0, 0)] * (len(lead) + 1) + [(0, slot - width)])
    return w.reshape(*lead, n_groups * slot)


def _pack_in_proj(w_in):
    offs = [0]
    for w in IN_WIDTHS:
        offs.append(offs[-1] + w)
    (dq, dk, dv, iq, ik, iw, sq, sk, sv, mq, mk, mv, cq, ckv, kr, gate) = [
        w_in[:, offs[n]:offs[n + 1]] for n in range(len(IN_WIDTHS))]
    vpad = lambda w: _pad_last(w, N_HEADS, HEAD_DIM, HEAD_SLOT)
    d = w_in.shape[0]
    kr_slot = jnp.concatenate([jnp.zeros((d, MLA_NOPE), F32), kr, jnp.zeros((d, HEAD_SLOT - MLA_NOPE - MLA_ROPE), F32)], axis=1)
    w1 = jnp.concatenate([
        dq, dk, vpad(dv),
        iq, jnp.tile(ik, (1, IDX_HEADS)), _pad_last(iw, 1, IDX_HEADS, LANES),
        sq, sk, vpad(sv),
        mq, mk, vpad(mv),
        cq, ckv, jnp.tile(kr_slot, (1, N_HEADS)),
    ], axis=1)
    assert w1.shape[1] == _G_END
    return w1.astype(BF16), gate.astype(BF16)


def _rope_tables(n_rot, seq):
    inv_freq = 1.0 / (ROPE_THETA ** (jnp.arange(0, n_rot, 2, dtype=F32) / n_rot))
    ang = jnp.arange(seq, dtype=F32)[:, None] * inv_freq[None, :]
    return jnp.cos(ang), jnp.sin(ang)


def _rope_lane_tables(n_rot, seq, lead, period, reps):
    cos, sin = _rope_tables(n_rot, seq)
    nh = n_rot // 2
    tail = period - lead - n_rot
    one = lambda n: jnp.ones((seq, n), F32)
    zero = lambda n: jnp.zeros((seq, n), F32)
    c = jnp.concatenate([one(lead), cos, cos, one(tail)], axis=1)
    s1 = jnp.concatenate([zero(lead), -sin, zero(nh), zero(tail)], axis=1)
    s2 = jnp.concatenate([zero(lead), zero(nh), sin, zero(tail)], axis=1)
    return tuple(jnp.tile(a, (1, reps)) for a in (c, s1, s2))


def kernel(x, norm_mix, w_in, b_gate, mla_q_norm, mla_kv_norm, w_uq, w_ukv, w_branch, w_out, norm_mlp, w_up,
           w_down, norm_final):
    bsz, seq, d = x.shape
    depth = w_in.shape[0]
    assert d == D_MODEL and seq % PROJ_ROWS == 0 and seq % ATT_T == 0 and seq // MOBA_BLOCK <= LANES
    n = bsz * seq
    tables = (_rope_lane_tables(PARTIAL_ROT, seq, 0, HEAD_DIM, N_HEADS)
              + _rope_lane_tables(IDX_ROT, seq, 0, IDX_DIM, IDX_HEADS)
              + _rope_lane_tables(MLA_ROPE, seq, MLA_NOPE, HEAD_SLOT, N_HEADS))
    row = lambda v: v.reshape(1, -1).astype(F32)
    nb = seq // MOBA_BLOCK
    x2 = x.reshape(n, d)
    for l in range(depth):
        w1, wg = _pack_in_proj(w_in[l])
        wuq = _pad_last(w_uq[l], N_HEADS, MLA_NOPE + MLA_ROPE, HEAD_SLOT).astype(BF16)
        ukv = w_ukv[l].reshape(MLA_KV_RANK, N_HEADS, 2 * HEAD_DIM)
        wukv = jnp.concatenate([
            _pad_last(ukv[:, :, :MLA_NOPE].reshape(MLA_KV_RANK, -1), N_HEADS, HEAD_DIM, HEAD_SLOT),
            _pad_last(ukv[:, :, MLA_NOPE:].reshape(MLA_KV_RANK, -1), N_HEADS, HEAD_DIM, HEAD_SLOT)], axis=1).astype(BF16)
        wb = jnp.pad(w_branch[l].reshape(N_BRANCH, N_HEADS, HEAD_DIM, D_MODEL),
                     ((0, 0), (0, 0), (0, HEAD_SLOT - HEAD_DIM), (0, 0))).reshape(N_BRANCH, VPAD_W, D_MODEL).astype(BF16)

        (qa, ka, va, qi, ki, iw, qb, kb, vb, qc, kc, vc, kmean, qd, kd, vd) = _proj_call(
            x2, row(norm_mix[l]), w1, row(mla_q_norm[l]), row(mla_kv_norm[l]), wuq, wukv, tables, seq)
        seqd = lambda a: a.reshape(bsz, seq, a.shape[-1])
        kmean = jnp.pad(kmean.reshape(bsz, nb, QKV_W), ((0, 0), (0, LANES - nb), (0, 0))).astype(BF16)

        oa = _dsa_call(seqd(qa), seqd(qi), seqd(iw), seqd(ka), seqd(va), seqd(ki))
        ob = _sb_call(seqd(qb), seqd(kb), seqd(vb))
        oc = _moba_call(seqd(qc), kmean, seqd(kc), seqd(vc))
        od = _mla_call(seqd(qd), seqd(kd), seqd(vd))

        flat = lambda a: a.reshape(n, VPAD_W)
        x2 = _merge_call(x2, row(norm_mix[l]), wg, b_gate[l].astype(F32), flat(oa), flat(ob), flat(oc), flat(od),
                         wb, w_out[l].astype(BF16))
        x2 = _mlp_call(x2, row(norm_mlp[l]), w_up[l].astype(BF16), w_down[l].astype(BF16), row(norm_final),
                       final_norm=(l == depth - 1))
    return x2.reshape(bsz, seq, d)
```

```python
import functools

import jax
import jax.numpy as jnp
from jax import lax
from jax.experimental import pallas as pl
from jax.experimental.pallas import tpu as pltpu

F32 = jnp.float32
BF16 = jnp.bfloat16
I32 = jnp.int32

D_MODEL = 1024
HEAD_DIM = 64
N_HEADS = 4
N_BRANCH = 4
ROPE_THETA = 500000.0
PARTIAL_ROT = HEAD_DIM // 4
NORM_EPS = 1e-6
IDX_HEADS = 8
IDX_DIM = 32
IDX_ROT = IDX_DIM // 4
DSA_TOPK_MAX = 256
MOBA_BLOCK = 256
MOBA_TOPK = 3
MLA_Q_RANK = 256
MLA_KV_RANK = 128
MLA_NOPE = 64
MLA_ROPE = 32
D_FF = 4 * D_MODEL
IN_WIDTHS = (256, 256, 256, 256, 32, 8, 256, 256, 256, 256, 256, 256, 256, 128, 32, 4096)

LANES = 128
HEAD_SLOT = LANES
QKV_W = N_HEADS * HEAD_DIM
VPAD_W = N_HEADS * HEAD_SLOT
ONES_LANE = HEAD_DIM
VMEM_LIMIT = 56 * 1024 * 1024

PROJ_ROWS = 512
ATT_T = 256
NEG = -1e30
INT_MIN = -(2 ** 31)
LOG2E = 1.4426950408889634


def _dot(a, b):
    return jnp.dot(a, b, preferred_element_type=F32)


def _rms(x, g):
    return x * lax.rsqrt(jnp.mean(x * x, axis=-1, keepdims=True) + NORM_EPS) * g


def _rope(x, c, s1, s2, shift):
    w = x.shape[-1]
    return x * c + pltpu.roll(x, w - shift, 1) * s1 + pltpu.roll(x, shift, 1) * s2


def _resident(shape, index_map):
    return pl.BlockSpec(shape, index_map, pipeline_mode=pl.Buffered(1))


def _params(*sem):
    return pltpu.CompilerParams(dimension_semantics=sem, vmem_limit_bytes=VMEM_LIMIT)


_G_A = 0
_G_I = 1024
_G_B = 1664
_G_C = 2688
_G_D = 3712
_G_END = 4608


def _proj_kernel(x_ref, g_ref, w_ref, gq_ref, gkv_ref, wuq_ref, wukv_ref,
                 cp_ref, s1p_ref, s2p_ref, ci_ref, s1i_ref, s2i_ref, cm_ref, s1m_ref, s2m_ref,
                 qa_ref, ka_ref, va_ref, qi_ref, ki_ref, iw_ref,
                 qb_ref, kb_ref, vb_ref, qc_ref, kc_ref, vc_ref, kmean_ref,
                 qd_ref, kd_ref, vd_ref):
    hb = _rms(x_ref[...], g_ref[...]).astype(BF16)
    tm = hb.shape[0]
    cp, s1p, s2p = cp_ref[...], s1p_ref[...], s2p_ref[...]
    ones_lane = (lax.broadcasted_iota(I32, (tm, VPAD_W), 1) % HEAD_SLOT) == ONES_LANE
    qscale = HEAD_DIM ** -0.5 * LOG2E

    def group(c0, width):
        return _dot(hb, w_ref[:, c0:c0 + width])

    z = group(_G_A, 1024)
    qa_ref[...] = (_rope(z[:, 0:256], cp, s1p, s2p, PARTIAL_ROT // 2) * qscale).T.astype(BF16)
    ka_ref[...] = _rope(z[:, 256:512], cp, s1p, s2p, PARTIAL_ROT // 2).astype(BF16)
    va_ref[...] = jnp.where(ones_lane, 1.0, z[:, 512:1024]).T.astype(BF16)

    z = group(_G_I, 640)
    ci, s1i, s2i = ci_ref[...], s1i_ref[...], s2i_ref[...]
    qi_ref[...] = _rope(z[:, 0:256], ci, s1i, s2i, IDX_ROT // 2).T.astype(BF16)
    ki_ref[...] = _rope(z[:, 256:512], ci, s1i, s2i, IDX_ROT // 2).astype(BF16)
    iw_ref[...] = (z[:, 512:640] * ((IDX_DIM ** -0.5) * (IDX_HEADS ** -0.5))).T[0:IDX_HEADS, :]

    z = group(_G_B, 1024)
    qb_ref[...] = (z[:, 0:256] * qscale).T.astype(BF16)
    kb_ref[...] = z[:, 256:512].astype(BF16)
    vb_ref[...] = z[:, 512:1024].T.astype(BF16)

    z = group(_G_C, 1024)
    qc_ref[...] = (_rope(z[:, 0:256], cp, s1p, s2p, PARTIAL_ROT // 2) * qscale).T.astype(BF16)
    kc = _rope(z[:, 256:512], cp, s1p, s2p, PARTIAL_ROT // 2)
    kc_ref[...] = kc.astype(BF16)
    vc_ref[...] = jnp.where(ones_lane, 1.0, z[:, 512:1024]).T.astype(BF16)
    kmean_ref[0] = jnp.sum(kc.reshape(tm // MOBA_BLOCK, MOBA_BLOCK, QKV_W), axis=1) * (1.0 / MOBA_BLOCK)

    z = group(_G_D, 896)
    cm, s1m, s2m = cm_ref[...], s1m_ref[...], s2m_ref[...]
    cq = _rms(z[:, 0:256], gq_ref[...]).astype(BF16)
    ckv = _rms(z[:, 256:384], gkv_ref[...]).astype(BF16)
    qu = _dot(cq, wuq_ref[...])
    mla_scale = (MLA_NOPE + MLA_ROPE) ** -0.5 * LOG2E
    qd_ref[...] = (_rope(qu, cm, s1m, s2m, MLA_ROPE // 2) * mla_scale).T.astype(BF16)
    kvu = _dot(ckv, wukv_ref[...])
    kd_ref[...] = (kvu[:, 0:512] + _rope(z[:, 384:896], cm, s1m, s2m, MLA_ROPE // 2)).astype(BF16)
    vd_ref[...] = jnp.where(ones_lane, 1.0, kvu[:, 512:1024]).T.astype(BF16)


def _proj_call(x2, g, w1, gq, gkv, wuq, wukv, tables, seq):
    n = x2.shape[0]
    tm = PROJ_ROWS
    nt = n // tm
    tps = seq // tm
    row = lambda w: pl.BlockSpec((tm, w), lambda i: (i, 0))
    tab = lambda w: pl.BlockSpec((tm, w), lambda i: (i % tps, 0))
    const = lambda a: _resident(a.shape, lambda i: (0,) * a.ndim)
    in_specs = [row(D_MODEL), const(g), const(w1), const(gq), const(gkv), const(wuq), const(wukv),
                tab(256), tab(256), tab(256), tab(256), tab(256), tab(256), tab(512), tab(512), tab(512)]
    kb = tm // MOBA_BLOCK
    out_shapes = []
    out_specs = []

    def add(width, dtype, transposed):
        if transposed:
            out_shapes.append(jax.ShapeDtypeStruct((width, n), dtype))
            out_specs.append(pl.BlockSpec((width, tm), lambda i: (0, i)))
        else:
            out_shapes.append(jax.ShapeDtypeStruct((n, width), dtype))
            out_specs.append(row(width))

    add(256, BF16, True); add(256, BF16, False); add(512, BF16, True)
    add(256, BF16, True); add(256, BF16, False); add(IDX_HEADS, F32, True)
    add(256, BF16, True); add(256, BF16, False); add(512, BF16, True)
    add(256, BF16, True); add(256, BF16, False); add(512, BF16, True)
    out_shapes.append(jax.ShapeDtypeStruct((nt, kb, QKV_W), F32))
    out_specs.append(pl.BlockSpec((1, kb, QKV_W), lambda i: (i, 0, 0)))
    add(512, BF16, True); add(512, BF16, False); add(512, BF16, True)
    return pl.pallas_call(
        _proj_kernel,
        grid=(nt,),
        in_specs=in_specs,
        out_specs=out_specs,
        out_shape=out_shapes,
        compiler_params=_params("arbitrary"),
        name="proj",
    )(x2, g, w1, gq, gkv, wuq, wukv, *tables)


def _stack_heads_t(q_t, n_heads, head_dim):
    q32 = q_t.astype(F32)
    sub_head = lax.broadcasted_iota(I32, q32.shape, 0) // head_dim
    return jnp.concatenate([jnp.where(sub_head == h, q32, 0.0) for h in range(n_heads)], axis=1).astype(BF16)


def _kblock(ref, j, t):
    return ref[pl.ds(pl.multiple_of(j * t, t), t), :]


def _vblock_t(v_t_ref, h, j, t):
    return v_t_ref[h * HEAD_SLOT:(h + 1) * HEAD_SLOT, pl.ds(pl.multiple_of(j * t, t), t)]


def _softmax_head(s, h, v_t_ref, j, m_ref, acc_ref, t):
    m_old = m_ref[...]
    m_new = jnp.maximum(m_old, jnp.max(s, axis=0, keepdims=True))
    alpha = jnp.exp2(m_old - m_new)
    p = jnp.exp2(s - m_new).astype(BF16)
    acc_ref[...] = alpha * acc_ref[...] + _dot(_vblock_t(v_t_ref, h, j, t), p)
    m_ref[...] = m_new


def _softmax_init(m_refs, acc_refs):
    for m_ref, acc_ref in zip(m_refs, acc_refs):
        m_ref[...] = jnp.full(m_ref.shape, NEG, F32)
        acc_ref[...] = jnp.zeros(acc_ref.shape, F32)


def _softmax_finish(o_ref, acc_refs):
    for h, acc_ref in enumerate(acc_refs):
        a = acc_ref[...]
        o = a / a[ONES_LANE:ONES_LANE + 1, :]
        o_ref[:, h * HEAD_SLOT:(h + 1) * HEAD_SLOT] = o.T.astype(BF16)


def _tile_ids(t):
    return lax.broadcasted_iota(I32, (t, t), 0), lax.broadcasted_iota(I32, (t, t), 1)


def _sweep(n, logits, consume, s_refs):
    s_refs[0][...] = logits(0)

    def pair(pi, carry):
        p = 2 * pi
        s_refs[1][...] = logits(p + 1)
        consume(s_refs[0], p, False)
        s_refs[0][...] = logits(p + 2)
        consume(s_refs[1], p + 1, False)
        return carry

    lax.fori_loop(0, (n - 1) // 2, pair, 0)

    @pl.when(n % 2 == 1)
    def _():
        consume(s_refs[0], n - 1, True)

    @pl.when(n % 2 == 0)
    def _():
        s_refs[1][...] = logits(n - 1)
        consume(s_refs[0], n - 2, False)
        consume(s_refs[1], n - 1, True)


def _att_specs(t, seq, q_heights, k_widths, n_vt):
    nq = seq // t
    qs = [pl.BlockSpec((w, t), lambda b, i: (0, b * nq + i)) for w in q_heights]
    ks = [_resident((seq, w), lambda b, i: (b, 0)) for w in k_widths]
    vs = [_resident((VPAD_W, seq), lambda b, i: (0, b)) for _ in range(n_vt)]
    return qs, ks, vs


def _att_out(n, seq, t):
    nq = seq // t
    return (jax.ShapeDtypeStruct((n, VPAD_W), BF16),
            pl.BlockSpec((t, VPAD_W), lambda b, i: (b * nq + i, 0)))


def _att_scratch(t):
    return ([pltpu.VMEM((t, N_HEADS * t), F32)] * 2 + [pltpu.VMEM((1, t), F32)] * N_HEADS
            + [pltpu.VMEM((HEAD_SLOT, t), F32)] * N_HEADS)


def _split_scratch(scratch):
    return scratch[0:2], scratch[2:2 + N_HEADS], scratch[2 + N_HEADS:2 + 2 * N_HEADS]


def _order_key(x):
    bits = pltpu.bitcast(x + 0.0, I32)
    return bits ^ ((bits >> 31) & 0x7FFFFFFF)


def _dsa_kernel(qa_ref, qi_ref, iw_ref, ka_ref, ki_ref, va_ref, o_ref, keys_ref, hi_ref, lo_ref, *scratch, t, topk):
    s_refs, m_refs, acc_refs = _split_scratch(scratch)
    i = pl.program_id(1)
    nblk = i + 1
    key_ids, qry_ids = _tile_ids(t)

    qis = _stack_heads_t(qi_ref[...], IDX_HEADS, IDX_DIM)
    iw = iw_ref[...]

    def score_tile(j, carry):
        r = _dot(_kblock(ki_ref, j, t), qis)
        sc = iw[0:1, :] * jnp.maximum(r[:, 0:t], 0.0)
        for h in range(1, IDX_HEADS):
            sc = sc + iw[h:h + 1, :] * jnp.maximum(r[:, h * t:(h + 1) * t], 0.0)
        key = _order_key(sc)
        key = jnp.where((j < i) | (key_ids <= qry_ids), key, INT_MIN)
        sl = pl.ds(pl.multiple_of(j * t, t), t)
        keys_ref[sl, :] = key
        hi_ref[sl, :] = (key >> 16).astype(jnp.int16)
        lo_ref[sl, :] = ((key & 0xFFFF) - 32768).astype(jnp.int16)
        return carry

    lax.fori_loop(0, nblk, score_tile, 0)

    def count16(ref, cand):
        cand16 = cand.astype(jnp.int16)

        def body(c, acc):
            tile = ref[pl.ds(pl.multiple_of(c * t, t), t), :]
            hit = jnp.where(tile >= cand16, jnp.int16(1), jnp.int16(0))
            parts = [hit[g * 16:(g + 1) * 16] for g in range(t // 16)]
            while len(parts) > 1:
                parts = [parts[a] + parts[a + 1] for a in range(0, len(parts), 2)]
            return acc + parts[0]
        acc = lax.fori_loop(0, nblk, body, jnp.zeros((16, t), jnp.int16))
        return jnp.sum(acc.astype(I32), axis=0, keepdims=True)

    def kth_largest16(ref, kth):
        c0 = count16(ref, jnp.zeros((1, t), I32))
        v0 = jnp.where(c0 >= kth, 0, -32768).astype(I32)

        def bit_step(b, v):
            cand = v | lax.shift_left(jnp.int32(1), 14 - b)
            return jnp.where(count16(ref, cand) >= kth, cand, v)
        return lax.fori_loop(0, 15, bit_step, v0)

    tau_hi = kth_largest16(hi_ref, jnp.full((1, t), topk, I32))
    above = jnp.where(tau_hi == 32767, 0, count16(hi_ref, jnp.minimum(tau_hi + 1, 32767)))
    tau_hi16 = tau_hi.astype(jnp.int16)

    def keep_low(c, carry):
        sl = pl.ds(pl.multiple_of(c * t, t), t)
        lo_ref[sl, :] = jnp.where(hi_ref[sl, :] == tau_hi16, lo_ref[sl, :], jnp.int16(-32768))
        return carry

    lax.fori_loop(0, nblk, keep_low, 0)
    tau_lo = kth_largest16(lo_ref, topk - above)
    tau = lax.shift_left(tau_hi, 16) | (tau_lo + 32768)
    tau = jnp.maximum(tau, INT_MIN + 1)

    def count(pred):
        def body(c, acc):
            hit = pred(keys_ref[pl.ds(pl.multiple_of(c * t, t), t), :]).astype(I32)
            part = hit[0:8]
            for g in range(1, t // 8):
                part = part + hit[g * 8:(g + 1) * 8]
            return acc + part
        acc = lax.fori_loop(0, nblk, body, jnp.zeros((8, t), I32))
        return jnp.sum(acc, axis=0, keepdims=True)

    c_ge = count(lambda k: k >= tau)

    @pl.when(jnp.max(c_ge) > topk)
    def _():
        c_gt = count(lambda k: k > tau)
        allowed = (topk - c_gt).astype(F32)
        prefix_incl = (qry_ids <= key_ids).astype(BF16)

        def fix(c, seen):
            sl = pl.ds(pl.multiple_of(c * t, t), t)
            k = keys_ref[sl, :]
            eq = k == tau
            rank = seen + _dot(prefix_incl, eq.astype(BF16))
            keys_ref[sl, :] = jnp.where(eq & (rank > allowed), tau - 1, k)
            return seen + jnp.sum(eq.astype(F32), axis=0, keepdims=True)

        lax.fori_loop(0, nblk, fix, jnp.zeros((1, t), F32))

    qs = _stack_heads_t(qa_ref[...], N_HEADS, HEAD_DIM)
    _softmax_init(m_refs, acc_refs)

    def consume(s_ref, p, last):
        sel = keys_ref[pl.ds(pl.multiple_of(p * t, t), t), :] >= tau
        for h in range(N_HEADS):
            s = jnp.where(sel, s_ref[:, h * t:(h + 1) * t], NEG)
            _softmax_head(s, h, va_ref, p, m_refs[h], acc_refs[h], t)

    _sweep(nblk, lambda p: _dot(_kblock(ka_ref, p, t), qs), consume, s_refs)
    _softmax_finish(o_ref, acc_refs)


def _dsa_call(qa_t, qi_t, iw_t, ka, ki, va_t, bsz, seq):
    t = ATT_T
    topk = min(DSA_TOPK_MAX, seq // 4)
    qs, ks, vs = _att_specs(t, seq, (QKV_W, QKV_W, IDX_HEADS), (QKV_W, QKV_W), 1)
    out_shape, out_spec = _att_out(bsz * seq, seq, t)
    return pl.pallas_call(
        functools.partial(_dsa_kernel, t=t, topk=topk),
        grid=(bsz, seq // t),
        in_specs=qs + ks + vs,
        out_specs=out_spec,
        out_shape=out_shape,
        scratch_shapes=[pltpu.VMEM((seq, t), I32), pltpu.VMEM((seq, t), jnp.int16), pltpu.VMEM((seq, t), jnp.int16)]
        + _att_scratch(t),
        compiler_params=_params("arbitrary", "arbitrary"),
        name="dsa",
    )(qa_t, qi_t, iw_t, ka, ki, va_t)


def _sb_kernel(q_ref, k_ref, v_ref, o_ref, *scratch, t):
    s_refs, carry_refs, acc_refs = _split_scratch(scratch)
    i = pl.program_id(1)
    qs = _stack_heads_t(q_ref[...], N_HEADS, HEAD_DIM)
    key_ids, qry_ids = _tile_ids(t)
    suffix = (qry_ids > key_ids).astype(BF16)
    for h in range(N_HEADS):
        carry_refs[h][...] = jnp.zeros(carry_refs[h].shape, F32)
        acc_refs[h][...] = jnp.zeros(acc_refs[h].shape, F32)

    def head_step(z, h, j, past):
        neg_abs = pltpu.bitcast(pltpu.bitcast(z, I32) | INT_MIN, F32)
        nlf = jnp.maximum(z, 0.0) + jnp.log2(1.0 + jnp.exp2(neg_abs))
        if past is not None:
            nlf = jnp.where(past, nlf, 0.0)
        later = _dot(suffix, nlf.astype(BF16))
        a = jnp.exp2(z - (nlf + (later + carry_refs[h][...])))
        if past is not None:
            a = jnp.where(past, a, 0.0)
        acc_refs[h][...] += _dot(_vblock_t(v_ref, h, j, t), a.astype(BF16))
        carry_refs[h][...] += jnp.sum(nlf, axis=0, keepdims=True)

    z_diag = _dot(_kblock(k_ref, i, t), qs)
    for h in range(N_HEADS):
        head_step(z_diag[:, h * t:(h + 1) * t], h, i, key_ids < qry_ids)

    def consume(s_ref, p, last):
        for h in range(N_HEADS):
            head_step(s_ref[:, h * t:(h + 1) * t], h, i - 1 - p, None)

    @pl.when(i > 0)
    def _():
        _sweep(i, lambda p: _dot(_kblock(k_ref, i - 1 - p, t), qs), consume, s_refs)

    for h in range(N_HEADS):
        o_ref[:, h * HEAD_SLOT:(h + 1) * HEAD_SLOT] = acc_refs[h][...].T.astype(BF16)


def _sb_call(q_t, k, v_t, bsz, seq):
    t = ATT_T
    qs, ks, vs = _att_specs(t, seq, (QKV_W,), (QKV_W,), 1)
    out_shape, out_spec = _att_out(bsz * seq, seq, t)
    return pl.pallas_call(
        functools.partial(_sb_kernel, t=t),
        grid=(bsz, seq // t),
        in_specs=qs + ks + vs,
        out_specs=out_spec,
        out_shape=out_shape,
        scratch_shapes=_att_scratch(t),
        compiler_params=_params("arbitrary", "arbitrary"),
        name="stickbreak",
    )(q_t, k, v_t)


def _moba_kernel(q_ref, kmean_ref, k_ref, v_ref, o_ref, chosen_ref, *scratch, t, n_sel):
    s_refs, m_refs, acc_refs = _split_scratch(scratch)
    i = pl.program_id(1)
    qs = _stack_heads_t(q_ref[...], N_HEADS, HEAD_DIM)
    cols_all = N_HEADS * t
    gate = _dot(kmean_ref[0], qs)
    blk = lax.broadcasted_iota(I32, (LANES, cols_all), 0)
    ninf = -jnp.inf
    g = jnp.where(blk < i, gate, ninf)
    chosen = jnp.zeros((LANES, cols_all), F32)
    for _ in range(n_sel):
        mx = jnp.max(g, axis=0, keepdims=True)
        first = jnp.min(jnp.where(g == mx, blk, LANES), axis=0, keepdims=True)
        pick = (blk == first) & (mx > ninf)
        chosen = jnp.where(pick, 1.0, chosen)
        g = jnp.where(pick, ninf, g)
    chosen_ref[...] = chosen
    _softmax_init(m_refs, acc_refs)
    key_ids, qry_ids = _tile_ids(t)

    def consume(s_ref, p, last):
        if last:
            keep = [key_ids <= qry_ids] * N_HEADS
        else:
            on = chosen_ref[pl.ds(p, 1), :] > 0.0
            keep = [on[:, h * t:(h + 1) * t] for h in range(N_HEADS)]
        for h in range(N_HEADS):
            s = jnp.where(keep[h], s_ref[:, h * t:(h + 1) * t], NEG)
            _softmax_head(s, h, v_ref, p, m_refs[h], acc_refs[h], t)

    _sweep(i + 1, lambda p: _dot(_kblock(k_ref, p, t), qs), consume, s_refs)
    _softmax_finish(o_ref, acc_refs)


def _moba_call(q_t, kmean, k, v_t, bsz, seq):
    t = ATT_T
    assert t == MOBA_BLOCK
    n_sel = min(MOBA_TOPK, seq // MOBA_BLOCK)
    qs, ks, vs = _att_specs(t, seq, (QKV_W,), (QKV_W,), 1)
    out_shape, out_spec = _att_out(bsz * seq, seq, t)
    return pl.pallas_call(
        functools.partial(_moba_kernel, t=t, n_sel=n_sel),
        grid=(bsz, seq // t),
        in_specs=qs + [_resident((1, LANES, QKV_W), lambda b, i: (b, 0, 0))] + ks + vs,
        out_specs=out_spec,
        out_shape=out_shape,
        scratch_shapes=[pltpu.VMEM((LANES, N_HEADS * t), F32)] + _att_scratch(t),
        compiler_params=_params("arbitrary", "arbitrary"),
        name="moba",
    )(q_t, kmean, k, v_t)


def _mla_kernel(q_ref, k_ref, v_ref, o_ref, *scratch, t):
    s_refs, m_refs, acc_refs = _split_scratch(scratch)
    i = pl.program_id(1)
    q_t = q_ref[...]
    _softmax_init(m_refs, acc_refs)
    key_ids, qry_ids = _tile_ids(t)

    def logits(p):
        kblk = _kblock(k_ref, p, t)
        return jnp.concatenate(
            [_dot(kblk[:, h * HEAD_SLOT:(h + 1) * HEAD_SLOT], q_t[h * HEAD_SLOT:(h + 1) * HEAD_SLOT, :])
             for h in range(N_HEADS)], axis=1)

    def consume(s_ref, p, last):
        for h in range(N_HEADS):
            s = s_ref[:, h * t:(h + 1) * t]
            if last:
                s = jnp.where(key_ids <= qry_ids, s, NEG)
            _softmax_head(s, h, v_ref, p, m_refs[h], acc_refs[h], t)

    _sweep(i + 1, logits, consume, s_refs)
    _softmax_finish(o_ref, acc_refs)


def _mla_call(q_t, k, v_t, bsz, seq):
    t = ATT_T
    qs, ks, vs = _att_specs(t, seq, (VPAD_W,), (VPAD_W,), 1)
    out_shape, out_spec = _att_out(bsz * seq, seq, t)
    return pl.pallas_call(
        functools.partial(_mla_kernel, t=t),
        grid=(bsz, seq // t),
        in_specs=qs + ks + vs,
        out_specs=out_spec,
        out_shape=out_shape,
        scratch_shapes=_att_scratch(t),
        compiler_params=_params("arbitrary", "arbitrary"),
        name="mla",
    )(q_t, k, v_t)


def _merge_kernel(x_ref, g_ref, wg_ref, bg_ref, oa_ref, ob_ref, oc_ref, od_ref, wb_ref, wout_ref, out_ref):
    x = x_ref[...]
    hb = _rms(x, g_ref[...]).astype(BF16)
    merged = None
    for n, o_ref in enumerate((oa_ref, ob_ref, oc_ref, od_ref)):
        logit = _dot(hb, wg_ref[:, n * D_MODEL:(n + 1) * D_MODEL]) + bg_ref[n:n + 1, :]
        gate = 1.0 / (1.0 + jnp.exp(-logit))
        term = gate * _dot(o_ref[...], wb_ref[n])
        merged = term if merged is None else merged + term
    out_ref[...] = x + _dot(merged.astype(BF16), wout_ref[...])


def _merge_call(x2, g, wg, bg, oa, ob, oc, od, wb, wout):
    n = x2.shape[0]
    tm = PROJ_ROWS
    row = lambda w: pl.BlockSpec((tm, w), lambda i: (i, 0))
    const = lambda a: _resident(a.shape, lambda i: (0,) * a.ndim)
    return pl.pallas_call(
        _merge_kernel,
        grid=(n // tm,),
        in_specs=[row(D_MODEL), const(g), const(wg), const(bg), row(VPAD_W), row(VPAD_W), row(VPAD_W),
                  row(VPAD_W), const(wb), const(wout)],
        out_specs=row(D_MODEL),
        out_shape=jax.ShapeDtypeStruct((n, D_MODEL), F32),
        compiler_params=_params("arbitrary"),
        name="merge",
    )(x2, g, wg, bg, oa, ob, oc, od, wb, wout)


def _mlp_kernel(x_ref, g_ref, wup_ref, wdown_ref, gf_ref, out_ref, *, final_norm):
    x = x_ref[...]
    hb = _rms(x, g_ref[...]).astype(BF16)
    y = x
    for c in range(D_FF // D_MODEL):
        u = jnp.maximum(_dot(hb, wup_ref[:, c * D_MODEL:(c + 1) * D_MODEL]), 0.0)
        y = y + _dot((u * u).astype(BF16), wdown_ref[c * D_MODEL:(c + 1) * D_MODEL, :])
    if final_norm:
        y = _rms(y, gf_ref[...])
    out_ref[...] = y


def _mlp_call(x2, g, wup, wdown, gf, final_norm):
    n = x2.shape[0]
    tm = PROJ_ROWS
    row = lambda w: pl.BlockSpec((tm, w), lambda i: (i, 0))
    const = lambda a: _resident(a.shape, lambda i: (0,) * a.ndim)
    return pl.pallas_call(
        functools.partial(_mlp_kernel, final_norm=final_norm),
        grid=(n // tm,),
        in_specs=[row(D_MODEL), const(g), const(wup), const(wdown), const(gf)],
        out_specs=row(D_MODEL),
        out_shape=jax.ShapeDtypeStruct((n, D_MODEL), F32),
        compiler_params=_params("arbitrary"),
        name="mlp",
    )(x2, g, wup, wdown, gf)


def _pad_last(w, n_groups, width, slot):
    lead = w.shape[:-1]
    w = w.reshape(*lead, n_groups, width)
    w = jnp.pad(w, [(0, 0)] * (len(lead) + 1) + [(0, slot - width)])
    return w.reshape(*lead, n_groups * slot)


def _pack_in_proj(w_in):
    offs = [0]
    for w in IN_WIDTHS:
        offs.append(offs[-1] + w)
    (dq, dk, dv, iq, ik, iw, sq, sk, sv, mq, mk, mv, cq, ckv, kr, gate) = [
        w_in[:, offs[n]:offs[n + 1]] for n in range(len(IN_WIDTHS))]
    vpad = lambda w: _pad_last(w, N_HEADS, HEAD_DIM, HEAD_SLOT)
    d = w_in.shape[0]
    kr_slot = jnp.concatenate([jnp.zeros((d, MLA_NOPE), F32), kr, jnp.zeros((d, HEAD_SLOT - MLA_NOPE - MLA_ROPE), F32)], axis=1)
    w1 = jnp.concatenate([
        dq, dk, vpad(dv),
        iq, jnp.tile(ik, (1, IDX_HEADS)), _pad_last(iw, 1, IDX_HEADS, LANES),
        sq, sk, vpad(sv),
        mq, mk, vpad(mv),
        cq, ckv, jnp.tile(kr_slot, (1, N_HEADS)),
    ], axis=1)
    assert w1.shape[1] == _G_END
    return w1.astype(BF16), gate.astype(BF16)


def _rope_tables(n_rot, seq):
    inv_freq = 1.0 / (ROPE_THETA ** (jnp.arange(0, n_rot, 2, dtype=F32) / n_rot))
    ang = jnp.arange(seq, dtype=F32)[:, None] * inv_freq[None, :]
    return jnp.cos(ang), jnp.sin(ang)


def _rope_lane_tables(n_rot, seq, lead, period, reps):
    cos, sin = _rope_tables(n_rot, seq)
    nh = n_rot // 2
    tail = period - lead - n_rot
    one = lambda n: jnp.ones((seq, n), F32)
    zero = lambda n: jnp.zeros((seq, n), F32)
    c = jnp.concatenate([one(lead), cos, cos, one(tail)], axis=1)
    s1 = jnp.concatenate([zero(lead), -sin, zero(nh), zero(tail)], axis=1)
    s2 = jnp.concatenate([zero(lead), zero(nh), sin, zero(tail)], axis=1)
    return tuple(jnp.tile(a, (1, reps)) for a in (c, s1, s2))


def kernel(x, norm_mix, w_in, b_gate, mla_q_norm, mla_kv_norm, w_uq, w_ukv, w_branch, w_out, norm_mlp, w_up,
           w_down, norm_final):
    bsz, seq, d = x.shape
    depth = w_in.shape[0]
    assert d == D_MODEL and seq % PROJ_ROWS == 0 and seq % ATT_T == 0 and seq // MOBA_BLOCK <= LANES
    n = bsz * seq
    tables = (_rope_lane_tables(PARTIAL_ROT, seq, 0, HEAD_DIM, N_HEADS)
              + _rope_lane_tables(IDX_ROT, seq, 0, IDX_DIM, IDX_HEADS)
              + _rope_lane_tables(MLA_ROPE, seq, MLA_NOPE, HEAD_SLOT, N_HEADS))
    row = lambda v: v.reshape(1, -1).astype(F32)
    nb = seq // MOBA_BLOCK
    x2 = x.reshape(n, d)
    for l in range(depth):
        w1, wg = _pack_in_proj(w_in[l])
        wuq = _pad_last(w_uq[l], N_HEADS, MLA_NOPE + MLA_ROPE, HEAD_SLOT).astype(BF16)
        ukv = w_ukv[l].reshape(MLA_KV_RANK, N_HEADS, 2 * HEAD_DIM)
        wukv = jnp.concatenate([
            _pad_last(ukv[:, :, :MLA_NOPE].reshape(MLA_KV_RANK, -1), N_HEADS, HEAD_DIM, HEAD_SLOT),
            _pad_last(ukv[:, :, MLA_NOPE:].reshape(MLA_KV_RANK, -1), N_HEADS, HEAD_DIM, HEAD_SLOT)], axis=1).astype(BF16)
        wb = jnp.pad(w_branch[l].reshape(N_BRANCH, N_HEADS, HEAD_DIM, D_MODEL),
                     ((0, 0), (0, 0), (0, HEAD_SLOT - HEAD_DIM), (0, 0))).reshape(N_BRANCH, VPAD_W, D_MODEL).astype(BF16)

        (qa, ka, va, qi, ki, iw, qb, kb, vb, qc, kc, vc, kmean, qd, kd, vd) = _proj_call(
            x2, row(norm_mix[l]), w1, row(mla_q_norm[l]), row(mla_kv_norm[l]), wuq, wukv, tables, seq)
        kmean = jnp.pad(kmean.reshape(bsz, nb, QKV_W), ((0, 0), (0, LANES - nb), (0, 0))).astype(BF16)

        oa = _dsa_call(qa, qi, iw, ka, ki, va, bsz, seq)
        ob = _sb_call(qb, kb, vb, bsz, seq)
        oc = _moba_call(qc, kmean, kc, vc, bsz, seq)
        od = _mla_call(qd, kd, vd, bsz, seq)

        x2 = _merge_call(x2, row(norm_mix[l]), wg, b_gate[l].astype(F32), oa, ob, oc, od, wb, w_out[l].astype(BF16))
        x2 = _mlp_call(x2, row(norm_mlp[l]), w_up[l].astype(BF16), w_down[l].astype(BF16), row(norm_final),
                       final_norm=(l == depth - 1))
    return x2.reshape(bsz, seq, d)
```

```python
import functools

import jax
import jax.numpy as jnp
from jax import lax
from jax.experimental import pallas as pl
from jax.experimental.pallas import tpu as pltpu

F32 = jnp.float32
BF16 = jnp.bfloat16
I32 = jnp.int32

D_MODEL = 1024
HEAD_DIM = 64
N_HEADS = 4
N_BRANCH = 4
ROPE_THETA = 500000.0
PARTIAL_ROT = HEAD_DIM // 4
NORM_EPS = 1e-6
IDX_HEADS = 8
IDX_DIM = 32
IDX_ROT = IDX_DIM // 4
DSA_TOPK_MAX = 256
MOBA_BLOCK = 256
MOBA_TOPK = 3
MLA_Q_RANK = 256
MLA_KV_RANK = 128
MLA_NOPE = 64
MLA_ROPE = 32
D_FF = 4 * D_MODEL
IN_WIDTHS = (256, 256, 256, 256, 32, 8, 256, 256, 256, 256, 256, 256, 256, 128, 32, 4096)

LANES = 128
HEAD_SLOT = LANES
QKV_W = N_HEADS * HEAD_DIM
VPAD_W = N_HEADS * HEAD_SLOT
ONES_LANE = HEAD_DIM
VMEM_LIMIT = 56 * 1024 * 1024

PROJ_ROWS = 512
ATT_T = 256
NEG = -1e30
INT_MIN = -(2 ** 31)
LOG2E = 1.4426950408889634


def _dot(a, b):
    return jnp.dot(a, b, preferred_element_type=F32)


def _rms(x, g):
    return x * lax.rsqrt(jnp.mean(x * x, axis=-1, keepdims=True) + NORM_EPS) * g


def _rope(x, c, s1, s2, shift):
    w = x.shape[-1]
    return x * c + pltpu.roll(x, w - shift, 1) * s1 + pltpu.roll(x, shift, 1) * s2


def _resident(shape, index_map):
    return pl.BlockSpec(shape, index_map, pipeline_mode=pl.Buffered(1))


def _params(*sem):
    return pltpu.CompilerParams(dimension_semantics=sem, vmem_limit_bytes=VMEM_LIMIT)


_G_A = 0
_G_I = 1024
_G_B = 1664
_G_C = 2688
_G_D = 3712
_G_END = 4608


def _proj_kernel(x_ref, g_ref, w_ref, gq_ref, gkv_ref, wuq_ref, wukv_ref,
                 cp_ref, s1p_ref, s2p_ref, ci_ref, s1i_ref, s2i_ref, cm_ref, s1m_ref, s2m_ref,
                 qa_ref, ka_ref, va_ref, qi_ref, ki_ref, iw_ref,
                 qb_ref, kb_ref, vb_ref, qc_ref, kc_ref, vc_ref, kmean_ref,
                 qd_ref, kd_ref, vd_ref):
    hb = _rms(x_ref[...], g_ref[...]).astype(BF16)
    tm = hb.shape[0]
    cp, s1p, s2p = cp_ref[...], s1p_ref[...], s2p_ref[...]
    ones_lane = (lax.broadcasted_iota(I32, (tm, VPAD_W), 1) % HEAD_SLOT) == ONES_LANE
    qscale = HEAD_DIM ** -0.5 * LOG2E

    def group(c0, width):
        return _dot(hb, w_ref[:, c0:c0 + width])

    z = group(_G_A, 1024)
    qa_ref[...] = (_rope(z[:, 0:256], cp, s1p, s2p, PARTIAL_ROT // 2) * qscale).T.astype(BF16)
    ka_ref[...] = _rope(z[:, 256:512], cp, s1p, s2p, PARTIAL_ROT // 2).astype(BF16)
    va_ref[...] = jnp.where(ones_lane, 1.0, z[:, 512:1024]).T.astype(BF16)

    z = group(_G_I, 640)
    ci, s1i, s2i = ci_ref[...], s1i_ref[...], s2i_ref[...]
    qi_ref[...] = _rope(z[:, 0:256], ci, s1i, s2i, IDX_ROT // 2).T.astype(BF16)
    ki_ref[...] = _rope(z[:, 256:512], ci, s1i, s2i, IDX_ROT // 2).astype(BF16)
    iw_ref[...] = (z[:, 512:640] * ((IDX_DIM ** -0.5) * (IDX_HEADS ** -0.5))).T[0:IDX_HEADS, :]

    z = group(_G_B, 1024)
    qb_ref[...] = (z[:, 0:256] * qscale).T.astype(BF16)
    kb_ref[...] = z[:, 256:512].astype(BF16)
    vb_ref[...] = z[:, 512:1024].T.astype(BF16)

    z = group(_G_C, 1024)
    qc_ref[...] = (_rope(z[:, 0:256], cp, s1p, s2p, PARTIAL_ROT // 2) * qscale).T.astype(BF16)
    kc = _rope(z[:, 256:512], cp, s1p, s2p, PARTIAL_ROT // 2)
    kc_ref[...] = kc.astype(BF16)
    vc_ref[...] = jnp.where(ones_lane, 1.0, z[:, 512:1024]).T.astype(BF16)
    kmean_ref[0] = jnp.sum(kc.reshape(tm // MOBA_BLOCK, MOBA_BLOCK, QKV_W), axis=1) * (1.0 / MOBA_BLOCK)

    z = group(_G_D, 896)
    cm, s1m, s2m = cm_ref[...], s1m_ref[...], s2m_ref[...]
    cq = _rms(z[:, 0:256], gq_ref[...]).astype(BF16)
    ckv = _rms(z[:, 256:384], gkv_ref[...]).astype(BF16)
    qu = _dot(cq, wuq_ref[...])
    mla_scale = (MLA_NOPE + MLA_ROPE) ** -0.5 * LOG2E
    qd_ref[...] = (_rope(qu, cm, s1m, s2m, MLA_ROPE // 2) * mla_scale).T.astype(BF16)
    kvu = _dot(ckv, wukv_ref[...])
    kd_ref[...] = (kvu[:, 0:512] + _rope(z[:, 384:896], cm, s1m, s2m, MLA_ROPE // 2)).astype(BF16)
    vd_ref[...] = jnp.where(ones_lane, 1.0, kvu[:, 512:1024]).T.astype(BF16)


def _proj_call(x2, g, w1, gq, gkv, wuq, wukv, tables, seq):
    n = x2.shape[0]
    tm = PROJ_ROWS
    nt = n // tm
    tps = seq // tm
    row = lambda w: pl.BlockSpec((tm, w), lambda i: (i, 0))
    tab = lambda w: pl.BlockSpec((tm, w), lambda i: (i % tps, 0))
    const = lambda a: _resident(a.shape, lambda i: (0,) * a.ndim)
    in_specs = [row(D_MODEL), const(g), const(w1), const(gq), const(gkv), const(wuq), const(wukv),
                tab(256), tab(256), tab(256), tab(256), tab(256), tab(256), tab(512), tab(512), tab(512)]
    kb = tm // MOBA_BLOCK
    out_shapes = []
    out_specs = []

    def add(width, dtype, transposed):
        if transposed:
            out_shapes.append(jax.ShapeDtypeStruct((width, n), dtype))
            out_specs.append(pl.BlockSpec((width, tm), lambda i: (0, i)))
        else:
            out_shapes.append(jax.ShapeDtypeStruct((n, width), dtype))
            out_specs.append(row(width))

    add(256, BF16, True); add(256, BF16, False); add(512, BF16, True)
    add(256, BF16, True); add(256, BF16, False); add(IDX_HEADS, F32, True)
    add(256, BF16, True); add(256, BF16, False); add(512, BF16, True)
    add(256, BF16, True); add(256, BF16, False); add(512, BF16, True)
    out_shapes.append(jax.ShapeDtypeStruct((nt, kb, QKV_W), F32))
    out_specs.append(pl.BlockSpec((1, kb, QKV_W), lambda i: (i, 0, 0)))
    add(512, BF16, True); add(512, BF16, False); add(512, BF16, True)
    return pl.pallas_call(
        _proj_kernel,
        grid=(nt,),
        in_specs=in_specs,
        out_specs=out_specs,
        out_shape=out_shapes,
        compiler_params=_params("arbitrary"),
        name="proj",
    )(x2, g, w1, gq, gkv, wuq, wukv, *tables)


def _stack_heads_t(q_t, n_heads, head_dim):
    q32 = q_t.astype(F32)
    sub_head = lax.broadcasted_iota(I32, q32.shape, 0) // head_dim
    return jnp.concatenate([jnp.where(sub_head == h, q32, 0.0) for h in range(n_heads)], axis=1).astype(BF16)


def _kblock(ref, j, t):
    return ref[pl.ds(pl.multiple_of(j * t, t), t), :]


def _vblock_t(v_t_ref, h, j, t):
    return v_t_ref[h * HEAD_SLOT:(h + 1) * HEAD_SLOT, pl.ds(pl.multiple_of(j * t, t), t)]


def _softmax_tile(s, v_t_ref, j, m_ref, acc_refs, t):
    m_old = m_ref[...]
    m_new = jnp.maximum(m_old, jnp.max(s, axis=0, keepdims=True))
    alpha = jnp.exp2(m_old - m_new)
    p = jnp.exp2(s - m_new).astype(BF16)
    for h, acc_ref in enumerate(acc_refs):
        cols = slice(h * t, (h + 1) * t)
        acc_ref[...] = alpha[:, cols] * acc_ref[...] + _dot(_vblock_t(v_t_ref, h, j, t), p[:, cols])
    m_ref[...] = m_new


def _softmax_init(m_ref, acc_refs):
    m_ref[...] = jnp.full(m_ref.shape, NEG, F32)
    for acc_ref in acc_refs:
        acc_ref[...] = jnp.zeros(acc_ref.shape, F32)


def _softmax_finish(o_ref, acc_refs):
    for h, acc_ref in enumerate(acc_refs):
        a = acc_ref[...]
        o = a / a[ONES_LANE:ONES_LANE + 1, :]
        o_ref[:, h * HEAD_SLOT:(h + 1) * HEAD_SLOT] = o.T.astype(BF16)


def _wide_causal(t, strict):
    keys = lax.broadcasted_iota(I32, (t, N_HEADS * t), 0)
    qrys = lax.broadcasted_iota(I32, (t, N_HEADS * t), 1) % t
    return keys < qrys if strict else keys <= qrys


def _tile_ids(t):
    return lax.broadcasted_iota(I32, (t, t), 0), lax.broadcasted_iota(I32, (t, t), 1)


def _sweep(n, logits, consume, s_refs):
    s_refs[0][...] = logits(0)

    def pair(pi, carry):
        p = 2 * pi
        s_refs[1][...] = logits(p + 1)
        consume(s_refs[0], p, False)
        s_refs[0][...] = logits(p + 2)
        consume(s_refs[1], p + 1, False)
        return carry

    lax.fori_loop(0, (n - 1) // 2, pair, 0)

    @pl.when(n % 2 == 1)
    def _():
        consume(s_refs[0], n - 1, True)

    @pl.when(n % 2 == 0)
    def _():
        s_refs[1][...] = logits(n - 1)
        consume(s_refs[0], n - 2, False)
        consume(s_refs[1], n - 1, True)


def _att_specs(t, seq, q_heights, k_widths, n_vt):
    nq = seq // t
    qs = [pl.BlockSpec((w, t), lambda b, i: (0, b * nq + i)) for w in q_heights]
    ks = [_resident((seq, w), lambda b, i: (b, 0)) for w in k_widths]
    vs = [_resident((VPAD_W, seq), lambda b, i: (0, b)) for _ in range(n_vt)]
    return qs, ks, vs


def _att_out(n, seq, t):
    nq = seq // t
    return (jax.ShapeDtypeStruct((n, VPAD_W), BF16),
            pl.BlockSpec((t, VPAD_W), lambda b, i: (b * nq + i, 0)))


def _att_scratch(t):
    return ([pltpu.VMEM((t, N_HEADS * t), F32)] * 2 + [pltpu.VMEM((1, N_HEADS * t), F32)]
            + [pltpu.VMEM((HEAD_SLOT, t), F32)] * N_HEADS)


def _split_scratch(scratch):
    return scratch[0:2], scratch[2], scratch[3:3 + N_HEADS]


def _order_key(x):
    bits = pltpu.bitcast(x + 0.0, I32)
    return bits ^ ((bits >> 31) & 0x7FFFFFFF)


def _dsa_kernel(qa_ref, qi_ref, iw_ref, ka_ref, ki_ref, va_ref, o_ref, keys_ref, hi_ref, lo_ref, *scratch, t, topk):
    s_refs, m_ref, acc_refs = _split_scratch(scratch)
    i = pl.program_id(1)
    nblk = i + 1
    key_ids, qry_ids = _tile_ids(t)

    qis = _stack_heads_t(qi_ref[...], IDX_HEADS, IDX_DIM)
    iw = iw_ref[...]

    def score_tile(j, carry):
        r = _dot(_kblock(ki_ref, j, t), qis)
        sc = iw[0:1, :] * jnp.maximum(r[:, 0:t], 0.0)
        for h in range(1, IDX_HEADS):
            sc = sc + iw[h:h + 1, :] * jnp.maximum(r[:, h * t:(h + 1) * t], 0.0)
        key = _order_key(sc)
        key = jnp.where((j < i) | (key_ids <= qry_ids), key, INT_MIN)
        sl = pl.ds(pl.multiple_of(j * t, t), t)
        keys_ref[sl, :] = key
        hi_ref[sl, :] = (key >> 16).astype(jnp.int16)
        lo_ref[sl, :] = ((key & 0xFFFF) - 32768).astype(jnp.int16)
        return carry

    lax.fori_loop(0, nblk, score_tile, 0)

    def count16(ref, cand):
        cand16 = cand.astype(jnp.int16)

        def body(c, acc):
            tile = ref[pl.ds(pl.multiple_of(c * t, t), t), :]
            hit = jnp.where(tile >= cand16, jnp.int16(1), jnp.int16(0))
            parts = [hit[g * 16:(g + 1) * 16] for g in range(t // 16)]
            while len(parts) > 1:
                parts = [parts[a] + parts[a + 1] for a in range(0, len(parts), 2)]
            return acc + parts[0]
        acc = lax.fori_loop(0, nblk, body, jnp.zeros((16, t), jnp.int16))
        return jnp.sum(acc.astype(I32), axis=0, keepdims=True)

    def kth_largest16(ref, kth):
        c0 = count16(ref, jnp.zeros((1, t), I32))
        v0 = jnp.where(c0 >= kth, 0, -32768).astype(I32)

        def bit_step(b, v):
            cand = v | lax.shift_left(jnp.int32(1), 14 - b)
            return jnp.where(count16(ref, cand) >= kth, cand, v)
        return lax.fori_loop(0, 15, bit_step, v0)

    tau_hi = kth_largest16(hi_ref, jnp.full((1, t), topk, I32))
    above = jnp.where(tau_hi == 32767, 0, count16(hi_ref, jnp.minimum(tau_hi + 1, 32767)))
    tau_hi16 = tau_hi.astype(jnp.int16)

    def keep_low(c, carry):
        sl = pl.ds(pl.multiple_of(c * t, t), t)
        lo_ref[sl, :] = jnp.where(hi_ref[sl, :] == tau_hi16, lo_ref[sl, :], jnp.int16(-32768))
        return carry

    lax.fori_loop(0, nblk, keep_low, 0)
    tau_lo = kth_largest16(lo_ref, topk - above)
    tau = lax.shift_left(tau_hi, 16) | (tau_lo + 32768)
    tau = jnp.maximum(tau, INT_MIN + 1)

    def count(pred):
        def body(c, acc):
            hit = pred(keys_ref[pl.ds(pl.multiple_of(c * t, t), t), :]).astype(I32)
            part = hit[0:8]
            for g in range(1, t // 8):
                part = part + hit[g * 8:(g + 1) * 8]
            return acc + part
        acc = lax.fori_loop(0, nblk, body, jnp.zeros((8, t), I32))
        return jnp.sum(acc, axis=0, keepdims=True)

    c_ge = count(lambda k: k >= tau)

    @pl.when(jnp.max(c_ge) > topk)
    def _():
        c_gt = count(lambda k: k > tau)
        allowed = (topk - c_gt).astype(F32)
        prefix_incl = (qry_ids <= key_ids).astype(BF16)

        def fix(c, seen):
            sl = pl.ds(pl.multiple_of(c * t, t), t)
            k = keys_ref[sl, :]
            eq = k == tau
            rank = seen + _dot(prefix_incl, eq.astype(BF16))
            keys_ref[sl, :] = jnp.where(eq & (rank > allowed), tau - 1, k)
            return seen + jnp.sum(eq.astype(F32), axis=0, keepdims=True)

        lax.fori_loop(0, nblk, fix, jnp.zeros((1, t), F32))

    qs = _stack_heads_t(qa_ref[...], N_HEADS, HEAD_DIM)
    _softmax_init(m_ref, acc_refs)

    def consume(s_ref, p, last):
        sel = keys_ref[pl.ds(pl.multiple_of(p * t, t), t), :] >= tau
        s = jnp.concatenate([jnp.where(sel, s_ref[:, h * t:(h + 1) * t], NEG) for h in range(N_HEADS)], axis=1)
        _softmax_tile(s, va_ref, p, m_ref, acc_refs, t)

    _sweep(nblk, lambda p: _dot(_kblock(ka_ref, p, t), qs), consume, s_refs)
    _softmax_finish(o_ref, acc_refs)


def _dsa_call(qa_t, qi_t, iw_t, ka, ki, va_t, bsz, seq):
    t = ATT_T
    topk = min(DSA_TOPK_MAX, seq // 4)
    qs, ks, vs = _att_specs(t, seq, (QKV_W, QKV_W, IDX_HEADS), (QKV_W, QKV_W), 1)
    out_shape, out_spec = _att_out(bsz * seq, seq, t)
    return pl.pallas_call(
        functools.partial(_dsa_kernel, t=t, topk=topk),
        grid=(bsz, seq // t),
        in_specs=qs + ks + vs,
        out_specs=out_spec,
        out_shape=out_shape,
        scratch_shapes=[pltpu.VMEM((seq, t), I32), pltpu.VMEM((seq, t), jnp.int16), pltpu.VMEM((seq, t), jnp.int16)]
        + _att_scratch(t),
        compiler_params=_params("arbitrary", "arbitrary"),
        name="dsa",
    )(qa_t, qi_t, iw_t, ka, ki, va_t)


def _sb_kernel(q_ref, k_ref, v_ref, o_ref, *scratch, t):
    s_refs, carry_ref, acc_refs = _split_scratch(scratch)
    i = pl.program_id(1)
    qs = _stack_heads_t(q_ref[...], N_HEADS, HEAD_DIM)
    key_ids, qry_ids = _tile_ids(t)
    suffix = (qry_ids > key_ids).astype(BF16)
    carry_ref[...] = jnp.zeros(carry_ref.shape, F32)
    for acc_ref in acc_refs:
        acc_ref[...] = jnp.zeros(acc_ref.shape, F32)

    def tile_step(z, j, past):
        neg_abs = pltpu.bitcast(pltpu.bitcast(z, I32) | INT_MIN, F32)
        nlf = jnp.maximum(z, 0.0) + jnp.log2(1.0 + jnp.exp2(neg_abs))
        if past is not None:
            nlf = jnp.where(past, nlf, 0.0)
        later = _dot(suffix, nlf.astype(BF16))
        a = jnp.exp2(z - (nlf + (later + carry_ref[...])))
        if past is not None:
            a = jnp.where(past, a, 0.0)
        a = a.astype(BF16)
        for h in range(N_HEADS):
            acc_refs[h][...] += _dot(_vblock_t(v_ref, h, j, t), a[:, h * t:(h + 1) * t])
        carry_ref[...] += jnp.sum(nlf, axis=0, keepdims=True)

    tile_step(_dot(_kblock(k_ref, i, t), qs), i, _wide_causal(t, strict=True))

    def consume(s_ref, p, last):
        tile_step(s_ref[...], i - 1 - p, None)

    @pl.when(i > 0)
    def _():
        _sweep(i, lambda p: _dot(_kblock(k_ref, i - 1 - p, t), qs), consume, s_refs)

    for h in range(N_HEADS):
        o_ref[:, h * HEAD_SLOT:(h + 1) * HEAD_SLOT] = acc_refs[h][...].T.astype(BF16)


def _sb_call(q_t, k, v_t, bsz, seq):
    t = ATT_T
    qs, ks, vs = _att_specs(t, seq, (QKV_W,), (QKV_W,), 1)
    out_shape, out_spec = _att_out(bsz * seq, seq, t)
    return pl.pallas_call(
        functools.partial(_sb_kernel, t=t),
        grid=(bsz, seq // t),
        in_specs=qs + ks + vs,
        out_specs=out_spec,
        out_shape=out_shape,
        scratch_shapes=_att_scratch(t),
        compiler_params=_params("arbitrary", "arbitrary"),
        name="stickbreak",
    )(q_t, k, v_t)


def _moba_kernel(q_ref, kmean_ref, k_ref, v_ref, o_ref, chosen_ref, *scratch, t, n_sel):
    s_refs, m_ref, acc_refs = _split_scratch(scratch)
    i = pl.program_id(1)
    qs = _stack_heads_t(q_ref[...], N_HEADS, HEAD_DIM)
    cols_all = N_HEADS * t
    gate = _dot(kmean_ref[0], qs)
    blk = lax.broadcasted_iota(I32, (LANES, cols_all), 0)
    ninf = -jnp.inf
    g = jnp.where(blk < i, gate, ninf)
    chosen = jnp.zeros((LANES, cols_all), F32)
    for _ in range(n_sel):
        mx = jnp.max(g, axis=0, keepdims=True)
        first = jnp.min(jnp.where(g == mx, blk, LANES), axis=0, keepdims=True)
        pick = (blk == first) & (mx > ninf)
        chosen = jnp.where(pick, 1.0, chosen)
        g = jnp.where(pick, ninf, g)
    chosen_ref[...] = chosen
    _softmax_init(m_ref, acc_refs)

    def consume(s_ref, p, last):
        if last:
            keep = _wide_causal(t, strict=False)
        else:
            keep = chosen_ref[pl.ds(p, 1), :] > 0.0
        _softmax_tile(jnp.where(keep, s_ref[...], NEG), v_ref, p, m_ref, acc_refs, t)

    _sweep(i + 1, lambda p: _dot(_kblock(k_ref, p, t), qs), consume, s_refs)
    _softmax_finish(o_ref, acc_refs)


def _moba_call(q_t, kmean, k, v_t, bsz, seq):
    t = ATT_T
    assert t == MOBA_BLOCK
    n_sel = min(MOBA_TOPK, seq // MOBA_BLOCK)
    qs, ks, vs = _att_specs(t, seq, (QKV_W,), (QKV_W,), 1)
    out_shape, out_spec = _att_out(bsz * seq, seq, t)
    return pl.pallas_call(
        functools.partial(_moba_kernel, t=t, n_sel=n_sel),
        grid=(bsz, seq // t),
        in_specs=qs + [_resident((1, LANES, QKV_W), lambda b, i: (b, 0, 0))] + ks + vs,
        out_specs=out_spec,
        out_shape=out_shape,
        scratch_shapes=[pltpu.VMEM((LANES, N_HEADS * t), F32)] + _att_scratch(t),
        compiler_params=_params("arbitrary", "arbitrary"),
        name="moba",
    )(q_t, kmean, k, v_t)


def _mla_kernel(q_ref, k_ref, v_ref, o_ref, *scratch, t):
    s_refs, m_ref, acc_refs = _split_scratch(scratch)
    i = pl.program_id(1)
    q_t = q_ref[...]
    _softmax_init(m_ref, acc_refs)

    def logits(p):
        kblk = _kblock(k_ref, p, t)
        return jnp.concatenate(
            [_dot(kblk[:, h * HEAD_SLOT:(h + 1) * HEAD_SLOT], q_t[h * HEAD_SLOT:(h + 1) * HEAD_SLOT, :])
             for h in range(N_HEADS)], axis=1)

    def consume(s_ref, p, last):
        s = s_ref[...]
        if last:
            s = jnp.where(_wide_causal(t, strict=False), s, NEG)
        _softmax_tile(s, v_ref, p, m_ref, acc_refs, t)

    _sweep(i + 1, logits, consume, s_refs)
    _softmax_finish(o_ref, acc_refs)


def _mla_call(q_t, k, v_t, bsz, seq):
    t = ATT_T
    qs, ks, vs = _att_specs(t, seq, (VPAD_W,), (VPAD_W,), 1)
    out_shape, out_spec = _att_out(bsz * seq, seq, t)
    return pl.pallas_call(
        functools.partial(_mla_kernel, t=t),
        grid=(bsz, seq // t),
        in_specs=qs + ks + vs,
        out_specs=out_spec,
        out_shape=out_shape,
        scratch_shapes=_att_scratch(t),
        compiler_params=_params("arbitrary", "arbitrary"),
        name="mla",
    )(q_t, k, v_t)


def _merge_kernel(x_ref, g_ref, wg_ref, bg_ref, oa_ref, ob_ref, oc_ref, od_ref, wb_ref, wout_ref, out_ref):
    x = x_ref[...]
    hb = _rms(x, g_ref[...]).astype(BF16)
    merged = None
    for n, o_ref in enumerate((oa_ref, ob_ref, oc_ref, od_ref)):
        logit = _dot(hb, wg_ref[:, n * D_MODEL:(n + 1) * D_MODEL]) + bg_ref[n:n + 1, :]
        gate = 1.0 / (1.0 + jnp.exp(-logit))
        term = gate * _dot(o_ref[...], wb_ref[n])
        merged = term if merged is None else merged + term
    out_ref[...] = x + _dot(merged.astype(BF16), wout_ref[...])


def _merge_call(x2, g, wg, bg, oa, ob, oc, od, wb, wout):
    n = x2.shape[0]
    tm = PROJ_ROWS
    row = lambda w: pl.BlockSpec((tm, w), lambda i: (i, 0))
    const = lambda a: _resident(a.shape, lambda i: (0,) * a.ndim)
    return pl.pallas_call(
        _merge_kernel,
        grid=(n // tm,),
        in_specs=[row(D_MODEL), const(g), const(wg), const(bg), row(VPAD_W), row(VPAD_W), row(VPAD_W),
                  row(VPAD_W), const(wb), const(wout)],
        out_specs=row(D_MODEL),
        out_shape=jax.ShapeDtypeStruct((n, D_MODEL), F32),
        compiler_params=_params("arbitrary"),
        name="merge",
    )(x2, g, wg, bg, oa, ob, oc, od, wb, wout)


def _mlp_kernel(x_ref, g_ref, wup_ref, wdown_ref, gf_ref, out_ref, *, final_norm):
    x = x_ref[...]
    hb = _rms(x, g_ref[...]).astype(BF16)
    y = x
    for c in range(D_FF // D_MODEL):
        u = jnp.maximum(_dot(hb, wup_ref[:, c * D_MODEL:(c + 1) * D_MODEL]), 0.0)
        y = y + _dot((u * u).astype(BF16), wdown_ref[c * D_MODEL:(c + 1) * D_MODEL, :])
    if final_norm:
        y = _rms(y, gf_ref[...])
    out_ref[...] = y


def _mlp_call(x2, g, wup, wdown, gf, final_norm):
    n = x2.shape[0]
    tm = PROJ_ROWS
    row = lambda w: pl.BlockSpec((tm, w), lambda i: (i, 0))
    const = lambda a: _resident(a.shape, lambda i: (0,) * a.ndim)
    return pl.pallas_call(
        functools.partial(_mlp_kernel, final_norm=final_norm),
        grid=(n // tm,),
        in_specs=[row(D_MODEL), const(g), const(wup), const(wdown), const(gf)],
        out_specs=row(D_MODEL),
        out_shape=jax.ShapeDtypeStruct((n, D_MODEL), F32),
        compiler_params=_params("arbitrary"),
        name="mlp",
    )(x2, g, wup, wdown, gf)


def _pad_last(w, n_groups, width, slot):
    lead = w.shape[:-1]
    w = w.reshape(*lead, n_groups, width)
    w = jnp.pad(w, [(0, 0)] * (len(lead) + 1) + [(0, slot - width)])
    return w.reshape(*lead, n_groups * slot)


def _pack_in_proj(w_in):
    offs = [0]
    for w in IN_WIDTHS:
        offs.append(offs[-1] + w)
    (dq, dk, dv, iq, ik, iw, sq, sk, sv, mq, mk, mv, cq, ckv, kr, gate) = [
        w_in[:, offs[n]:offs[n + 1]] for n in range(len(IN_WIDTHS))]
    vpad = lambda w: _pad_last(w, N_HEADS, HEAD_DIM, HEAD_SLOT)
    d = w_in.shape[0]
    kr_slot = jnp.concatenate([jnp.zeros((d, MLA_NOPE), F32), kr, jnp.zeros((d, HEAD_SLOT - MLA_NOPE - MLA_ROPE), F32)], axis=1)
    w1 = jnp.concatenate([
        dq, dk, vpad(dv),
        iq, jnp.tile(ik, (1, IDX_HEADS)), _pad_last(iw, 1, IDX_HEADS, LANES),
        sq, sk, vpad(sv),
        mq, mk, vpad(mv),
        cq, ckv, jnp.tile(kr_slot, (1, N_HEADS)),
    ], axis=1)
    assert w1.shape[1] == _G_END
    return w1.astype(BF16), gate.astype(BF16)


def _rope_tables(n_rot, seq):
    inv_freq = 1.0 / (ROPE_THETA ** (jnp.arange(0, n_rot, 2, dtype=F32) / n_rot))
    ang = jnp.arange(seq, dtype=F32)[:, None] * inv_freq[None, :]
    return jnp.cos(ang), jnp.sin(ang)


def _rope_lane_tables(n_rot, seq, lead, period, reps):
    cos, sin = _rope_tables(n_rot, seq)
    nh = n_rot // 2
    tail = period - lead - n_rot
    one = lambda n: jnp.ones((seq, n), F32)
    zero = lambda n: jnp.zeros((seq, n), F32)
    c = jnp.concatenate([one(lead), cos, cos, one(tail)], axis=1)
    s1 = jnp.concatenate([zero(lead), -sin, zero(nh), zero(tail)], axis=1)
    s2 = jnp.concatenate([zero(lead), zero(nh), sin, zero(tail)], axis=1)
    return tuple(jnp.tile(a, (1, reps)) for a in (c, s1, s2))


def kernel(x, norm_mix, w_in, b_gate, mla_q_norm, mla_kv_norm, w_uq, w_ukv, w_branch, w_out, norm_mlp, w_up,
           w_down, norm_final):
    bsz, seq, d = x.shape
    depth = w_in.shape[0]
    assert d == D_MODEL and seq % PROJ_ROWS == 0 and seq % ATT_T == 0 and seq // MOBA_BLOCK <= LANES
    n = bsz * seq
    tables = (_rope_lane_tables(PARTIAL_ROT, seq, 0, HEAD_DIM, N_HEADS)
              + _rope_lane_tables(IDX_ROT, seq, 0, IDX_DIM, IDX_HEADS)
              + _rope_lane_tables(MLA_ROPE, seq, MLA_NOPE, HEAD_SLOT, N_HEADS))
    row = lambda v: v.reshape(1, -1).astype(F32)
    nb = seq // MOBA_BLOCK
    x2 = x.reshape(n, d)
    for l in range(depth):
        w1, wg = _pack_in_proj(w_in[l])
        wuq = _pad_last(w_uq[l], N_HEADS, MLA_NOPE + MLA_ROPE, HEAD_SLOT).astype(BF16)
        ukv = w_ukv[l].reshape(MLA_KV_RANK, N_HEADS, 2 * HEAD_DIM)
        wukv = jnp.concatenate([
            _pad_last(ukv[:, :, :MLA_NOPE].reshape(MLA_KV_RANK, -1), N_HEADS, HEAD_DIM, HEAD_SLOT),
            _pad_last(ukv[:, :, MLA_NOPE:].reshape(MLA_KV_RANK, -1), N_HEADS, HEAD_DIM, HEAD_SLOT)], axis=1).astype(BF16)
        wb = jnp.pad(w_branch[l].reshape(N_BRANCH, N_HEADS, HEAD_DIM, D_MODEL),
                     ((0, 0), (0, 0), (0, HEAD_SLOT - HEAD_DIM), (0, 0))).reshape(N_BRANCH, VPAD_W, D_MODEL).astype(BF16)

        (qa, ka, va, qi, ki, iw, qb, kb, vb, qc, kc, vc, kmean, qd, kd, vd) = _proj_call(
            x2, row(norm_mix[l]), w1, row(mla_q_norm[l]), row(mla_kv_norm[l]), wuq, wukv, tables, seq)
        kmean = jnp.pad(kmean.reshape(bsz, nb, QKV_W), ((0, 0), (0, LANES - nb), (0, 0))).astype(BF16)

        oa = _dsa_call(qa, qi, iw, ka, ki, va, bsz, seq)
        ob = _sb_call(qb, kb, vb, bsz, seq)
        oc = _moba_call(qc, kmean, kc, vc, bsz, seq)
        od = _mla_call(qd, kd, vd, bsz, seq)

        x2 = _merge_call(x2, row(norm_mix[l]), wg, b_gate[l].astype(F32), oa, ob, oc, od, wb, w_out[l].astype(BF16))
        x2 = _mlp_call(x2, row(norm_mlp[l]), w_up[l].astype(BF16), w_down[l].astype(BF16), row(norm_final),
                       final_norm=(l == depth - 1))
    return x2.reshape(bsz, seq, d)
```

```python
import functools

import jax
import jax.numpy as jnp
from jax import lax
from jax.experimental import pallas as pl
from jax.experimental.pallas import tpu as pltpu

F32 = jnp.float32
BF16 = jnp.bfloat16
I32 = jnp.int32

D_MODEL = 1024
HEAD_DIM = 64
N_HEADS = 4
N_BRANCH = 4
ROPE_THETA = 500000.0
PARTIAL_ROT = HEAD_DIM // 4
NORM_EPS = 1e-6
IDX_HEADS = 8
IDX_DIM = 32
IDX_ROT = IDX_DIM // 4
DSA_TOPK_MAX = 256
MOBA_BLOCK = 256
MOBA_TOPK = 3
MLA_Q_RANK = 256
MLA_KV_RANK = 128
MLA_NOPE = 64
MLA_ROPE = 32
D_FF = 4 * D_MODEL
IN_WIDTHS = (256, 256, 256, 256, 32, 8, 256, 256, 256, 256, 256, 256, 256, 128, 32, 4096)

LANES = 128
HEAD_SLOT = LANES
QKV_W = N_HEADS * HEAD_DIM
VPAD_W = N_HEADS * HEAD_SLOT
ONES_LANE = HEAD_DIM
VMEM_LIMIT = 56 * 1024 * 1024

PROJ_ROWS = 512
ATT_T = 256
NEG = -1e30
INT_MIN = -(2 ** 31)
LOG2E = 1.4426950408889634


def _dot(a, b):
    return jnp.dot(a, b, preferred_element_type=F32)


def _rms(x, g):
    return x * lax.rsqrt(jnp.mean(x * x, axis=-1, keepdims=True) + NORM_EPS) * g


def _rope(x, c, s1, s2, shift):
    w = x.shape[-1]
    return x * c + pltpu.roll(x, w - shift, 1) * s1 + pltpu.roll(x, shift, 1) * s2


def _resident(shape, index_map):
    return pl.BlockSpec(shape, index_map, pipeline_mode=pl.Buffered(1))


def _params(*sem):
    return pltpu.CompilerParams(dimension_semantics=sem, vmem_limit_bytes=VMEM_LIMIT)


_G_A = 0
_G_I = 1024
_G_B = 1664
_G_C = 2688
_G_D = 3712
_G_END = 4608


def _proj_kernel(x_ref, g_ref, w_ref, gq_ref, gkv_ref, wuq_ref, wukv_ref,
                 cp_ref, s1p_ref, s2p_ref, ci_ref, s1i_ref, s2i_ref, cm_ref, s1m_ref, s2m_ref,
                 qa_ref, ka_ref, va_ref, qi_ref, ki_ref, iw_ref,
                 qb_ref, kb_ref, vb_ref, qc_ref, kc_ref, vc_ref, kmean_ref,
                 qd_ref, kd_ref, vd_ref):
    hb = _rms(x_ref[...], g_ref[...]).astype(BF16)
    tm = hb.shape[0]
    cp, s1p, s2p = cp_ref[...], s1p_ref[...], s2p_ref[...]
    ones_lane = (lax.broadcasted_iota(I32, (tm, VPAD_W), 1) % HEAD_SLOT) == ONES_LANE
    qscale = HEAD_DIM ** -0.5 * LOG2E

    def group(c0, width):
        return _dot(hb, w_ref[:, c0:c0 + width])

    z = group(_G_A, 1024)
    qa_ref[...] = (_rope(z[:, 0:256], cp, s1p, s2p, PARTIAL_ROT // 2) * qscale).T.astype(BF16)
    ka_ref[...] = _rope(z[:, 256:512], cp, s1p, s2p, PARTIAL_ROT // 2).astype(BF16)
    va_ref[...] = jnp.where(ones_lane, 1.0, z[:, 512:1024]).T.astype(BF16)

    z = group(_G_I, 640)
    ci, s1i, s2i = ci_ref[...], s1i_ref[...], s2i_ref[...]
    qi_ref[...] = _rope(z[:, 0:256], ci, s1i, s2i, IDX_ROT // 2).T.astype(BF16)
    ki_ref[...] = _rope(z[:, 256:512], ci, s1i, s2i, IDX_ROT // 2).astype(BF16)
    iw_ref[...] = (z[:, 512:640] * ((IDX_DIM ** -0.5) * (IDX_HEADS ** -0.5))).T[0:IDX_HEADS, :]

    z = group(_G_B, 1024)
    qb_ref[...] = (z[:, 0:256] * qscale).T.astype(BF16)
    kb_ref[...] = z[:, 256:512].astype(BF16)
    vb_ref[...] = z[:, 512:1024].T.astype(BF16)

    z = group(_G_C, 1024)
    qc_ref[...] = (_rope(z[:, 0:256], cp, s1p, s2p, PARTIAL_ROT // 2) * qscale).T.astype(BF16)
    kc = _rope(z[:, 256:512], cp, s1p, s2p, PARTIAL_ROT // 2)
    kc_ref[...] = kc.astype(BF16)
    vc_ref[...] = jnp.where(ones_lane, 1.0, z[:, 512:1024]).T.astype(BF16)
    kmean_ref[0] = jnp.sum(kc.reshape(tm // MOBA_BLOCK, MOBA_BLOCK, QKV_W), axis=1) * (1.0 / MOBA_BLOCK)

    z = group(_G_D, 896)
    cm, s1m, s2m = cm_ref[...], s1m_ref[...], s2m_ref[...]
    cq = _rms(z[:, 0:256], gq_ref[...]).astype(BF16)
    ckv = _rms(z[:, 256:384], gkv_ref[...]).astype(BF16)
    qu = _dot(cq, wuq_ref[...])
    mla_scale = (MLA_NOPE + MLA_ROPE) ** -0.5 * LOG2E
    qd_ref[...] = (_rope(qu, cm, s1m, s2m, MLA_ROPE // 2) * mla_scale).T.astype(BF16)
    kvu = _dot(ckv, wukv_ref[...])
    kd_ref[...] = (kvu[:, 0:512] + _rope(z[:, 384:896], cm, s1m, s2m, MLA_ROPE // 2)).astype(BF16)
    vd_ref[...] = jnp.where(ones_lane, 1.0, kvu[:, 512:1024]).T.astype(BF16)


def _proj_call(x2, g, w1, gq, gkv, wuq, wukv, tables, seq):
    n = x2.shape[0]
    tm = PROJ_ROWS
    nt = n // tm
    tps = seq // tm
    row = lambda w: pl.BlockSpec((tm, w), lambda i: (i, 0))
    tab = lambda w: pl.BlockSpec((tm, w), lambda i: (i % tps, 0))
    const = lambda a: _resident(a.shape, lambda i: (0,) * a.ndim)
    in_specs = [row(D_MODEL), const(g), const(w1), const(gq), const(gkv), const(wuq), const(wukv),
                tab(256), tab(256), tab(256), tab(256), tab(256), tab(256), tab(512), tab(512), tab(512)]
    kb = tm // MOBA_BLOCK
    out_shapes = []
    out_specs = []

    def add(width, dtype, transposed):
        if transposed:
            out_shapes.append(jax.ShapeDtypeStruct((width, n), dtype))
            out_specs.append(pl.BlockSpec((width, tm), lambda i: (0, i)))
        else:
            out_shapes.append(jax.ShapeDtypeStruct((n, width), dtype))
            out_specs.append(row(width))

    add(256, BF16, True); add(256, BF16, False); add(512, BF16, True)
    add(256, BF16, True); add(256, BF16, False); add(IDX_HEADS, F32, True)
    add(256, BF16, True); add(256, BF16, False); add(512, BF16, True)
    add(256, BF16, True); add(256, BF16, False); add(512, BF16, True)
    out_shapes.append(jax.ShapeDtypeStruct((nt, kb, QKV_W), F32))
    out_specs.append(pl.BlockSpec((1, kb, QKV_W), lambda i: (i, 0, 0)))
    add(512, BF16, True); add(512, BF16, False); add(512, BF16, True)
    return pl.pallas_call(
        _proj_kernel,
        grid=(nt,),
        in_specs=in_specs,
        out_specs=out_specs,
        out_shape=out_shapes,
        compiler_params=_params("arbitrary"),
        name="proj",
    )(x2, g, w1, gq, gkv, wuq, wukv, *tables)


def _stack_heads_t(q_t, n_heads, head_dim):
    q32 = q_t.astype(F32)
    sub_head = lax.broadcasted_iota(I32, q32.shape, 0) // head_dim
    return jnp.concatenate([jnp.where(sub_head == h, q32, 0.0) for h in range(n_heads)], axis=1).astype(BF16)


def _kblock(ref, j, t):
    return ref[pl.ds(pl.multiple_of(j * t, t), t), :]


def _vblock_t(v_t_ref, h, j, t):
    return v_t_ref[h * HEAD_SLOT:(h + 1) * HEAD_SLOT, pl.ds(pl.multiple_of(j * t, t), t)]


def _softmax_tile(s, v_t_ref, j, m_ref, acc_refs, t):
    m_old = m_ref[...]
    m_new = jnp.maximum(m_old, jnp.max(s, axis=0, keepdims=True))
    alpha = jnp.exp2(m_old - m_new)
    p = jnp.exp2(s - m_new).astype(BF16)
    for h, acc_ref in enumerate(acc_refs):
        cols = slice(h * t, (h + 1) * t)
        acc_ref[...] = alpha[:, cols] * acc_ref[...] + _dot(_vblock_t(v_t_ref, h, j, t), p[:, cols])
    m_ref[...] = m_new


def _softmax_init(m_ref, acc_refs):
    m_ref[...] = jnp.full(m_ref.shape, NEG, F32)
    for acc_ref in acc_refs:
        acc_ref[...] = jnp.zeros(acc_ref.shape, F32)


def _softmax_finish(o_ref, acc_refs):
    for h, acc_ref in enumerate(acc_refs):
        a = acc_ref[...]
        o = a / a[ONES_LANE:ONES_LANE + 1, :]
        o_ref[:, h * HEAD_SLOT:(h + 1) * HEAD_SLOT] = o.T.astype(BF16)


def _wide_causal(t, strict):
    keys = lax.broadcasted_iota(I32, (t, N_HEADS * t), 0)
    qrys = lax.broadcasted_iota(I32, (t, N_HEADS * t), 1) % t
    return keys < qrys if strict else keys <= qrys


def _tile_ids(t):
    return lax.broadcasted_iota(I32, (t, t), 0), lax.broadcasted_iota(I32, (t, t), 1)


def _sweep(n, logits, consume, s_refs):
    s_refs[0][...] = logits(0)

    def pair(pi, carry):
        p = 2 * pi
        s_refs[1][...] = logits(p + 1)
        consume(s_refs[0], p, False)
        s_refs[0][...] = logits(p + 2)
        consume(s_refs[1], p + 1, False)
        return carry

    lax.fori_loop(0, (n - 1) // 2, pair, 0)

    @pl.when(n % 2 == 1)
    def _():
        consume(s_refs[0], n - 1, True)

    @pl.when(n % 2 == 0)
    def _():
        s_refs[1][...] = logits(n - 1)
        consume(s_refs[0], n - 2, False)
        consume(s_refs[1], n - 1, True)


def _att_specs(t, seq, q_heights, k_widths, n_vt):
    nq = seq // t
    qs = [pl.BlockSpec((w, t), lambda b, i: (0, b * nq + i)) for w in q_heights]
    ks = [_resident((seq, w), lambda b, i: (b, 0)) for w in k_widths]
    vs = [_resident((VPAD_W, seq), lambda b, i: (0, b)) for _ in range(n_vt)]
    return qs, ks, vs


def _att_out(n, seq, t):
    nq = seq // t
    return (jax.ShapeDtypeStruct((n, VPAD_W), BF16),
            pl.BlockSpec((t, VPAD_W), lambda b, i: (b * nq + i, 0)))


def _att_scratch(t):
    return ([pltpu.VMEM((t, N_HEADS * t), F32)] * 2 + [pltpu.VMEM((1, N_HEADS * t), F32)]
            + [pltpu.VMEM((HEAD_SLOT, t), F32)] * N_HEADS)


def _split_scratch(scratch):
    return scratch[0:2], scratch[2], scratch[3:3 + N_HEADS]


def _order_key(x):
    bits = pltpu.bitcast(x + 0.0, I32)
    return bits ^ ((bits >> 31) & 0x7FFFFFFF)


def _dsa_kernel(qa_ref, qi_ref, iw_ref, ka_ref, ki_ref, va_ref, o_ref, keys_ref, hi_ref, lo_ref, r0_ref, r1_ref,
                *scratch, t, topk):
    s_refs, m_ref, acc_refs = _split_scratch(scratch)
    i = pl.program_id(1)
    nblk = i + 1
    key_ids, qry_ids = _tile_ids(t)

    qis = _stack_heads_t(qi_ref[...], IDX_HEADS, IDX_DIM)
    iw = iw_ref[...]

    def score_tile(r_ref, j, last):
        sc = iw[0:1, :] * jnp.maximum(r_ref[:, 0:t], 0.0)
        for h in range(1, IDX_HEADS):
            sc = sc + iw[h:h + 1, :] * jnp.maximum(r_ref[:, h * t:(h + 1) * t], 0.0)
        key = _order_key(sc)
        if last:
            key = jnp.where(key_ids <= qry_ids, key, INT_MIN)
        sl = pl.ds(pl.multiple_of(j * t, t), t)
        keys_ref[sl, :] = key
        hi_ref[sl, :] = (key >> 16).astype(jnp.int16)
        lo_ref[sl, :] = ((key & 0xFFFF) - 32768).astype(jnp.int16)

    _sweep(nblk, lambda p: _dot(_kblock(ki_ref, p, t), qis), score_tile, (r0_ref, r1_ref))

    def count16(ref, cand):
        cand16 = cand.astype(jnp.int16)

        def tile_hits(c):
            tile = ref[pl.ds(pl.multiple_of(c * t, t), t), :]
            hit = jnp.where(tile >= cand16, jnp.int16(1), jnp.int16(0))
            return [hit[g * 16:(g + 1) * 16] for g in range(t // 16)]

        def tree_sum(parts):
            while len(parts) > 1:
                parts = [parts[a] + parts[a + 1] for a in range(0, len(parts), 2)]
            return parts[0]

        def pair(c, acc):
            return acc + tree_sum(tile_hits(2 * c) + tile_hits(2 * c + 1))

        acc = lax.fori_loop(0, nblk // 2, pair, jnp.zeros((16, t), jnp.int16))
        acc = lax.cond(nblk % 2 == 1, lambda a: a + tree_sum(tile_hits(nblk - 1)), lambda a: a, acc)
        return jnp.sum(acc.astype(I32), axis=0, keepdims=True)

    def kth_largest16(ref, kth):
        c0 = count16(ref, jnp.zeros((1, t), I32))
        v0 = jnp.where(c0 >= kth, 0, -32768).astype(I32)

        def bit_step(b, v):
            cand = v | lax.shift_left(jnp.int32(1), 14 - b)
            return jnp.where(count16(ref, cand) >= kth, cand, v)
        return lax.fori_loop(0, 15, bit_step, v0)

    tau_hi = kth_largest16(hi_ref, jnp.full((1, t), topk, I32))
    above = jnp.where(tau_hi == 32767, 0, count16(hi_ref, jnp.minimum(tau_hi + 1, 32767)))
    tau_hi16 = tau_hi.astype(jnp.int16)

    def keep_low(c, carry):
        sl = pl.ds(pl.multiple_of(c * t, t), t)
        lo_ref[sl, :] = jnp.where(hi_ref[sl, :] == tau_hi16, lo_ref[sl, :], jnp.int16(-32768))
        return carry

    lax.fori_loop(0, nblk, keep_low, 0)
    tau_lo = kth_largest16(lo_ref, topk - above)
    tau = lax.shift_left(tau_hi, 16) | (tau_lo + 32768)
    tau = jnp.maximum(tau, INT_MIN + 1)

    def count(pred):
        def body(c, acc):
            hit = pred(keys_ref[pl.ds(pl.multiple_of(c * t, t), t), :]).astype(I32)
            part = hit[0:8]
            for g in range(1, t // 8):
                part = part + hit[g * 8:(g + 1) * 8]
            return acc + part
        acc = lax.fori_loop(0, nblk, body, jnp.zeros((8, t), I32))
        return jnp.sum(acc, axis=0, keepdims=True)

    c_ge = count(lambda k: k >= tau)

    @pl.when(jnp.max(c_ge) > topk)
    def _():
        c_gt = count(lambda k: k > tau)
        allowed = (topk - c_gt).astype(F32)
        prefix_incl = (qry_ids <= key_ids).astype(BF16)

        def fix(c, seen):
            sl = pl.ds(pl.multiple_of(c * t, t), t)
            k = keys_ref[sl, :]
            eq = k == tau
            rank = seen + _dot(prefix_incl, eq.astype(BF16))
            keys_ref[sl, :] = jnp.where(eq & (rank > allowed), tau - 1, k)
            return seen + jnp.sum(eq.astype(F32), axis=0, keepdims=True)

        lax.fori_loop(0, nblk, fix, jnp.zeros((1, t), F32))

    qs = _stack_heads_t(qa_ref[...], N_HEADS, HEAD_DIM)
    _softmax_init(m_ref, acc_refs)

    def consume(s_ref, p, last):
        sel = keys_ref[pl.ds(pl.multiple_of(p * t, t), t), :] >= tau
        s = jnp.concatenate([jnp.where(sel, s_ref[:, h * t:(h + 1) * t], NEG) for h in range(N_HEADS)], axis=1)
        _softmax_tile(s, va_ref, p, m_ref, acc_refs, t)

    _sweep(nblk, lambda p: _dot(_kblock(ka_ref, p, t), qs), consume, s_refs)
    _softmax_finish(o_ref, acc_refs)


def _dsa_call(qa_t, qi_t, iw_t, ka, ki, va_t, bsz, seq):
    t = ATT_T
    topk = min(DSA_TOPK_MAX, seq // 4)
    qs, ks, vs = _att_specs(t, seq, (QKV_W, QKV_W, IDX_HEADS), (QKV_W, QKV_W), 1)
    out_shape, out_spec = _att_out(bsz * seq, seq, t)
    return pl.pallas_call(
        functools.partial(_dsa_kernel, t=t, topk=topk),
        grid=(bsz, seq // t),
        in_specs=qs + ks + vs,
        out_specs=out_spec,
        out_shape=out_shape,
        scratch_shapes=[pltpu.VMEM((seq, t), I32), pltpu.VMEM((seq, t), jnp.int16), pltpu.VMEM((seq, t), jnp.int16)]
        + [pltpu.VMEM((t, IDX_HEADS * t), F32)] * 2 + _att_scratch(t),
        compiler_params=_params("arbitrary", "arbitrary"),
        name="dsa",
    )(qa_t, qi_t, iw_t, ka, ki, va_t)


def _sb_kernel(q_ref, k_ref, v_ref, o_ref, *scratch, t):
    s_refs, carry_ref, acc_refs = _split_scratch(scratch)
    i = pl.program_id(1)
    qs = _stack_heads_t(q_ref[...], N_HEADS, HEAD_DIM)
    key_ids, qry_ids = _tile_ids(t)
    suffix = (qry_ids > key_ids).astype(BF16)
    carry_ref[...] = jnp.zeros(carry_ref.shape, F32)
    for acc_ref in acc_refs:
        acc_ref[...] = jnp.zeros(acc_ref.shape, F32)

    def tile_step(z, j, past):
        neg_abs = pltpu.bitcast(pltpu.bitcast(z, I32) | INT_MIN, F32)
        nlf = jnp.maximum(z, 0.0) + jnp.log2(1.0 + jnp.exp2(neg_abs))
        if past is not None:
            nlf = jnp.where(past, nlf, 0.0)
        later = _dot(suffix, nlf.astype(BF16))
        a = jnp.exp2(z - (nlf + (later + carry_ref[...])))
        if past is not None:
            a = jnp.where(past, a, 0.0)
        a = a.astype(BF16)
        for h in range(N_HEADS):
            acc_refs[h][...] += _dot(_vblock_t(v_ref, h, j, t), a[:, h * t:(h + 1) * t])
        carry_ref[...] += jnp.sum(nlf, axis=0, keepdims=True)

    tile_step(_dot(_kblock(k_ref, i, t), qs), i, _wide_causal(t, strict=True))

    def consume(s_ref, p, last):
        tile_step(s_ref[...], i - 1 - p, None)

    @pl.when(i > 0)
    def _():
        _sweep(i, lambda p: _dot(_kblock(k_ref, i - 1 - p, t), qs), consume, s_refs)

    for h in range(N_HEADS):
        o_ref[:, h * HEAD_SLOT:(h + 1) * HEAD_SLOT] = acc_refs[h][...].T.astype(BF16)


def _sb_call(q_t, k, v_t, bsz, seq):
    t = ATT_T
    qs, ks, vs = _att_specs(t, seq, (QKV_W,), (QKV_W,), 1)
    out_shape, out_spec = _att_out(bsz * seq, seq, t)
    return pl.pallas_call(
        functools.partial(_sb_kernel, t=t),
        grid=(bsz, seq // t),
        in_specs=qs + ks + vs,
        out_specs=out_spec,
        out_shape=out_shape,
        scratch_shapes=_att_scratch(t),
        compiler_params=_params("arbitrary", "arbitrary"),
        name="stickbreak",
    )(q_t, k, v_t)


def _moba_kernel(q_ref, kmean_ref, k_ref, v_ref, o_ref, chosen_ref, *scratch, t, n_sel):
    s_refs, m_ref, acc_refs = _split_scratch(scratch)
    i = pl.program_id(1)
    qs = _stack_heads_t(q_ref[...], N_HEADS, HEAD_DIM)
    cols_all = N_HEADS * t
    gate = _dot(kmean_ref[0], qs)
    blk = lax.broadcasted_iota(I32, (LANES, cols_all), 0)
    ninf = -jnp.inf
    g = jnp.where(blk < i, gate, ninf)
    chosen = jnp.zeros((LANES, cols_all), F32)
    for _ in range(n_sel):
        mx = jnp.max(g, axis=0, keepdims=True)
        first = jnp.min(jnp.where(g == mx, blk, LANES), axis=0, keepdims=True)
        pick = (blk == first) & (mx > ninf)
        chosen = jnp.where(pick, 1.0, chosen)
        g = jnp.where(pick, ninf, g)
    chosen_ref[...] = chosen
    _softmax_init(m_ref, acc_refs)

    def consume(s_ref, p, last):
        if last:
            keep = _wide_causal(t, strict=False)
        else:
            keep = chosen_ref[pl.ds(p, 1), :] > 0.0
        _softmax_tile(jnp.where(keep, s_ref[...], NEG), v_ref, p, m_ref, acc_refs, t)

    _sweep(i + 1, lambda p: _dot(_kblock(k_ref, p, t), qs), consume, s_refs)
    _softmax_finish(o_ref, acc_refs)


def _moba_call(q_t, kmean, k, v_t, bsz, seq):
    t = ATT_T
    assert t == MOBA_BLOCK
    n_sel = min(MOBA_TOPK, seq // MOBA_BLOCK)
    qs, ks, vs = _att_specs(t, seq, (QKV_W,), (QKV_W,), 1)
    out_shape, out_spec = _att_out(bsz * seq, seq, t)
    return pl.pallas_call(
        functools.partial(_moba_kernel, t=t, n_sel=n_sel),
        grid=(bsz, seq // t),
        in_specs=qs + [_resident((1, LANES, QKV_W), lambda b, i: (b, 0, 0))] + ks + vs,
        out_specs=out_spec,
        out_shape=out_shape,
        scratch_shapes=[pltpu.VMEM((LANES, N_HEADS * t), F32)] + _att_scratch(t),
        compiler_params=_params("arbitrary", "arbitrary"),
        name="moba",
    )(q_t, kmean, k, v_t)


def _mla_kernel(q_ref, k_ref, v_ref, o_ref, *scratch, t):
    s_refs, m_ref, acc_refs = _split_scratch(scratch)
    i = pl.program_id(1)
    q_t = q_ref[...]
    _softmax_init(m_ref, acc_refs)

    def logits(p):
        kblk = _kblock(k_ref, p, t)
        return jnp.concatenate(
            [_dot(kblk[:, h * HEAD_SLOT:(h + 1) * HEAD_SLOT], q_t[h * HEAD_SLOT:(h + 1) * HEAD_SLOT, :])
             for h in range(N_HEADS)], axis=1)

    def consume(s_ref, p, last):
        s = s_ref[...]
        if last:
            s = jnp.where(_wide_causal(t, strict=False), s, NEG)
        _softmax_tile(s, v_ref, p, m_ref, acc_refs, t)

    _sweep(i + 1, logits, consume, s_refs)
    _softmax_finish(o_ref, acc_refs)


def _mla_call(q_t, k, v_t, bsz, seq):
    t = ATT_T
    qs, ks, vs = _att_specs(t, seq, (VPAD_W,), (VPAD_W,), 1)
    out_shape, out_spec = _att_out(bsz * seq, seq, t)
    return pl.pallas_call(
        functools.partial(_mla_kernel, t=t),
        grid=(bsz, seq // t),
        in_specs=qs + ks + vs,
        out_specs=out_spec,
        out_shape=out_shape,
        scratch_shapes=_att_scratch(t),
        compiler_params=_params("arbitrary", "arbitrary"),
        name="mla",
    )(q_t, k, v_t)


def _merge_kernel(x_ref, g_ref, wg_ref, bg_ref, oa_ref, ob_ref, oc_ref, od_ref, wb_ref, wout_ref, out_ref):
    x = x_ref[...]
    hb = _rms(x, g_ref[...]).astype(BF16)
    merged = None
    for n, o_ref in enumerate((oa_ref, ob_ref, oc_ref, od_ref)):
        logit = _dot(hb, wg_ref[:, n * D_MODEL:(n + 1) * D_MODEL]) + bg_ref[n:n + 1, :]
        gate = 1.0 / (1.0 + jnp.exp(-logit))
        term = gate * _dot(o_ref[...], wb_ref[n])
        merged = term if merged is None else merged + term
    out_ref[...] = x + _dot(merged.astype(BF16), wout_ref[...])


def _merge_call(x2, g, wg, bg, oa, ob, oc, od, wb, wout):
    n = x2.shape[0]
    tm = PROJ_ROWS
    row = lambda w: pl.BlockSpec((tm, w), lambda i: (i, 0))
    const = lambda a: _resident(a.shape, lambda i: (0,) * a.ndim)
    return pl.pallas_call(
        _merge_kernel,
        grid=(n // tm,),
        in_specs=[row(D_MODEL), const(g), const(wg), const(bg), row(VPAD_W), row(VPAD_W), row(VPAD_W),
                  row(VPAD_W), const(wb), const(wout)],
        out_specs=row(D_MODEL),
        out_shape=jax.ShapeDtypeStruct((n, D_MODEL), F32),
        compiler_params=_params("arbitrary"),
        name="merge",
    )(x2, g, wg, bg, oa, ob, oc, od, wb, wout)


def _mlp_kernel(x_ref, g_ref, wup_ref, wdown_ref, gf_ref, out_ref, *, final_norm):
    x = x_ref[...]
    hb = _rms(x, g_ref[...]).astype(BF16)
    y = x
    for c in range(D_FF // D_MODEL):
        u = jnp.maximum(_dot(hb, wup_ref[:, c * D_MODEL:(c + 1) * D_MODEL]), 0.0)
        y = y + _dot((u * u).astype(BF16), wdown_ref[c * D_MODEL:(c + 1) * D_MODEL, :])
    if final_norm:
        y = _rms(y, gf_ref[...])
    out_ref[...] = y


def _mlp_call(x2, g, wup, wdown, gf, final_norm):
    n = x2.shape[0]
    tm = PROJ_ROWS
    row = lambda w: pl.BlockSpec((tm, w), lambda i: (i, 0))
    const = lambda a: _resident(a.shape, lambda i: (0,) * a.ndim)
    return pl.pallas_call(
        functools.partial(_mlp_kernel, final_norm=final_norm),
        grid=(n // tm,),
        in_specs=[row(D_MODEL), const(g), const(wup), const(wdown), const(gf)],
        out_specs=row(D_MODEL),
        out_shape=jax.ShapeDtypeStruct((n, D_MODEL), F32),
        compiler_params=_params("arbitrary"),
        name="mlp",
    )(x2, g, wup, wdown, gf)


def _pad_last(w, n_groups, width, slot):
    lead = w.shape[:-1]
    w = w.reshape(*lead, n_groups, width)
    w = jnp.pad(w, [(0, 0)] * (len(lead) + 1) + [(0, slot - width)])
    return w.reshape(*lead, n_groups * slot)


def _pack_in_proj(w_in):
    offs = [0]
    for w in IN_WIDTHS:
        offs.append(offs[-1] + w)
    (dq, dk, dv, iq, ik, iw, sq, sk, sv, mq, mk, mv, cq, ckv, kr, gate) = [
        w_in[:, offs[n]:offs[n + 1]] for n in range(len(IN_WIDTHS))]
    vpad = lambda w: _pad_last(w, N_HEADS, HEAD_DIM, HEAD_SLOT)
    d = w_in.shape[0]
    kr_slot = jnp.concatenate([jnp.zeros((d, MLA_NOPE), F32), kr, jnp.zeros((d, HEAD_SLOT - MLA_NOPE - MLA_ROPE), F32)], axis=1)
    w1 = jnp.concatenate([
        dq, dk, vpad(dv),
        iq, jnp.tile(ik, (1, IDX_HEADS)), _pad_last(iw, 1, IDX_HEADS, LANES),
        sq, sk, vpad(sv),
        mq, mk, vpad(mv),
        cq, ckv, jnp.tile(kr_slot, (1, N_HEADS)),
    ], axis=1)
    assert w1.shape[1] == _G_END
    return w1.astype(BF16), gate.astype(BF16)


def _rope_tables(n_rot, seq):
    inv_freq = 1.0 / (ROPE_THETA ** (jnp.arange(0, n_rot, 2, dtype=F32) / n_rot))
    ang = jnp.arange(seq, dtype=F32)[:, None] * inv_freq[None, :]
    return jnp.cos(ang), jnp.sin(ang)


def _rope_lane_tables(n_rot, seq, lead, period, reps):
    cos, sin = _rope_tables(n_rot, seq)
    nh = n_rot // 2
    tail = period - lead - n_rot
    one = lambda n: jnp.ones((seq, n), F32)
    zero = lambda n: jnp.zeros((seq, n), F32)
    c = jnp.concatenate([one(lead), cos, cos, one(tail)], axis=1)
    s1 = jnp.concatenate([zero(lead), -sin, zero(nh), zero(tail)], axis=1)
    s2 = jnp.concatenate([zero(lead), zero(nh), sin, zero(tail)], axis=1)
    return tuple(jnp.tile(a, (1, reps)) for a in (c, s1, s2))


def kernel(x, norm_mix, w_in, b_gate, mla_q_norm, mla_kv_norm, w_uq, w_ukv, w_branch, w_out, norm_mlp, w_up,
           w_down, norm_final):
    bsz, seq, d = x.shape
    depth = w_in.shape[0]
    assert d == D_MODEL and seq % PROJ_ROWS == 0 and seq % ATT_T == 0 and seq // MOBA_BLOCK <= LANES
    n = bsz * seq
    tables = (_rope_lane_tables(PARTIAL_ROT, seq, 0, HEAD_DIM, N_HEADS)
              + _rope_lane_tables(IDX_ROT, seq, 0, IDX_DIM, IDX_HEADS)
              + _rope_lane_tables(MLA_ROPE, seq, MLA_NOPE, HEAD_SLOT, N_HEADS))
    row = lambda v: v.reshape(1, -1).astype(F32)
    nb = seq // MOBA_BLOCK
    x2 = x.reshape(n, d)
    for l in range(depth):
        w1, wg = _pack_in_proj(w_in[l])
        wuq = _pad_last(w_uq[l], N_HEADS, MLA_NOPE + MLA_ROPE, HEAD_SLOT).astype(BF16)
        ukv = w_ukv[l].reshape(MLA_KV_RANK, N_HEADS, 2 * HEAD_DIM)
        wukv = jnp.concatenate([
            _pad_last(ukv[:, :, :MLA_NOPE].reshape(MLA_KV_RANK, -1), N_HEADS, HEAD_DIM, HEAD_SLOT),
            _pad_last(ukv[:, :, MLA_NOPE:].reshape(MLA_KV_RANK, -1), N_HEADS, HEAD_DIM, HEAD_SLOT)], axis=1).astype(BF16)
        wb = jnp.pad(w_branch[l].reshape(N_BRANCH, N_HEADS, HEAD_DIM, D_MODEL),
                     ((0, 0), (0, 0), (0, HEAD_SLOT - HEAD_DIM), (0, 0))).reshape(N_BRANCH, VPAD_W, D_MODEL).astype(BF16)

        (qa, ka, va, qi, ki, iw, qb, kb, vb, qc, kc, vc, kmean, qd, kd, vd) = _proj_call(
            x2, row(norm_mix[l]), w1, row(mla_q_norm[l]), row(mla_kv_norm[l]), wuq, wukv, tables, seq)
        kmean = jnp.pad(kmean.reshape(bsz, nb, QKV_W), ((0, 0), (0, LANES - nb), (0, 0))).astype(BF16)

        oa = _dsa_call(qa, qi, iw, ka, ki, va, bsz, seq)
        ob = _sb_call(qb, kb, vb, bsz, seq)
        oc = _moba_call(qc, kmean, kc, vc, bsz, seq)
        od = _mla_call(qd, kd, vd, bsz, seq)

        x2 = _merge_call(x2, row(norm_mix[l]), wg, b_gate[l].astype(F32), oa, ob, oc, od, wb, w_out[l].astype(BF16))
        x2 = _mlp_call(x2, row(norm_mlp[l]), w_up[l].astype(BF16), w_down[l].astype(BF16), row(norm_final),
                       final_norm=(l == depth - 1))
    return x2.reshape(bsz, seq, d)
```

```python
import functools

import jax
import jax.numpy as jnp
from jax import lax
from jax.experimental import pallas as pl
from jax.experimental.pallas import tpu as pltpu

F32 = jnp.float32
BF16 = jnp.bfloat16
I32 = jnp.int32

D_MODEL = 1024
HEAD_DIM = 64
N_HEADS = 4
N_BRANCH = 4
ROPE_THETA = 500000.0
PARTIAL_ROT = HEAD_DIM // 4
NORM_EPS = 1e-6
IDX_HEADS = 8
IDX_DIM = 32
IDX_ROT = IDX_DIM // 4
DSA_TOPK_MAX = 256
MOBA_BLOCK = 256
MOBA_TOPK = 3
MLA_Q_RANK = 256
MLA_KV_RANK = 128
MLA_NOPE = 64
MLA_ROPE = 32
D_FF = 4 * D_MODEL
IN_WIDTHS = (256, 256, 256, 256, 32, 8, 256, 256, 256, 256, 256, 256, 256, 128, 32, 4096)

LANES = 128
HEAD_SLOT = LANES
QKV_W = N_HEADS * HEAD_DIM
VPAD_W = N_HEADS * HEAD_SLOT
ONES_LANE = HEAD_DIM
VMEM_LIMIT = 56 * 1024 * 1024

PROJ_ROWS = 512
ATT_T = 256
NEG = -1e30
INT_MIN = -(2 ** 31)
LOG2E = 1.4426950408889634


def _dot(a, b):
    return jnp.dot(a, b, preferred_element_type=F32)


def _rms(x, g):
    return x * lax.rsqrt(jnp.mean(x * x, axis=-1, keepdims=True) + NORM_EPS) * g


def _rope(x, c, s1, s2, shift):
    w = x.shape[-1]
    return x * c + pltpu.roll(x, w - shift, 1) * s1 + pltpu.roll(x, shift, 1) * s2


def _resident(shape, index_map):
    return pl.BlockSpec(shape, index_map, pipeline_mode=pl.Buffered(1))


def _params(*sem):
    return pltpu.CompilerParams(dimension_semantics=sem, vmem_limit_bytes=VMEM_LIMIT)


_G_A = 0
_G_I = 1024
_G_B = 1664
_G_C = 2688
_G_D = 3712
_G_END = 4608


def _proj_kernel(x_ref, g_ref, w_ref, gq_ref, gkv_ref, wuq_ref, wukv_ref,
                 cp_ref, s1p_ref, s2p_ref, ci_ref, s1i_ref, s2i_ref, cm_ref, s1m_ref, s2m_ref,
                 qa_ref, ka_ref, va_ref, qi_ref, ki_ref, iw_ref,
                 qb_ref, kb_ref, vb_ref, qc_ref, kc_ref, vc_ref, kmean_ref,
                 qd_ref, kd_ref, vd_ref):
    hb = _rms(x_ref[...], g_ref[...]).astype(BF16)
    tm = hb.shape[0]
    cp, s1p, s2p = cp_ref[...], s1p_ref[...], s2p_ref[...]
    ones_lane = (lax.broadcasted_iota(I32, (tm, VPAD_W), 1) % HEAD_SLOT) == ONES_LANE
    qscale = HEAD_DIM ** -0.5 * LOG2E

    def group(c0, width):
        return _dot(hb, w_ref[:, c0:c0 + width])

    z = group(_G_A, 1024)
    qa_ref[...] = (_rope(z[:, 0:256], cp, s1p, s2p, PARTIAL_ROT // 2) * qscale).T.astype(BF16)
    ka_ref[...] = _rope(z[:, 256:512], cp, s1p, s2p, PARTIAL_ROT // 2).astype(BF16)
    va_ref[...] = jnp.where(ones_lane, 1.0, z[:, 512:1024]).T.astype(BF16)

    z = group(_G_I, 640)
    ci, s1i, s2i = ci_ref[...], s1i_ref[...], s2i_ref[...]
    qi_ref[...] = _rope(z[:, 0:256], ci, s1i, s2i, IDX_ROT // 2).T.astype(BF16)
    ki_ref[...] = _rope(z[:, 256:512], ci, s1i, s2i, IDX_ROT // 2).astype(BF16)
    iw_ref[...] = (z[:, 512:640] * ((IDX_DIM ** -0.5) * (IDX_HEADS ** -0.5))).T[0:IDX_HEADS, :]

    z = group(_G_B, 1024)
    qb_ref[...] = (z[:, 0:256] * qscale).T.astype(BF16)
    kb_ref[...] = z[:, 256:512].astype(BF16)
    vb_ref[...] = z[:, 512:1024].T.astype(BF16)

    z = group(_G_C, 1024)
    qc_ref[...] = (_rope(z[:, 0:256], cp, s1p, s2p, PARTIAL_ROT // 2) * qscale).T.astype(BF16)
    kc = _rope(z[:, 256:512], cp, s1p, s2p, PARTIAL_ROT // 2)
    kc_ref[...] = kc.astype(BF16)
    vc_ref[...] = jnp.where(ones_lane, 1.0, z[:, 512:1024]).T.astype(BF16)
    kmean_ref[0] = jnp.sum(kc.reshape(tm // MOBA_BLOCK, MOBA_BLOCK, QKV_W), axis=1) * (1.0 / MOBA_BLOCK)

    z = group(_G_D, 896)
    cm, s1m, s2m = cm_ref[...], s1m_ref[...], s2m_ref[...]
    cq = _rms(z[:, 0:256], gq_ref[...]).astype(BF16)
    ckv = _rms(z[:, 256:384], gkv_ref[...]).astype(BF16)
    qu = _dot(cq, wuq_ref[...])
    mla_scale = (MLA_NOPE + MLA_ROPE) ** -0.5 * LOG2E
    qd_ref[...] = (_rope(qu, cm, s1m, s2m, MLA_ROPE // 2) * mla_scale).T.astype(BF16)
    kvu = _dot(ckv, wukv_ref[...])
    kd_ref[...] = (kvu[:, 0:512] + _rope(z[:, 384:896], cm, s1m, s2m, MLA_ROPE // 2)).astype(BF16)
    vd_ref[...] = jnp.where(ones_lane, 1.0, kvu[:, 512:1024]).T.astype(BF16)


def _proj_call(x2, g, w1, gq, gkv, wuq, wukv, tables, seq):
    n = x2.shape[0]
    tm = PROJ_ROWS
    nt = n // tm
    tps = seq // tm
    row = lambda w: pl.BlockSpec((tm, w), lambda i: (i, 0))
    tab = lambda w: pl.BlockSpec((tm, w), lambda i: (i % tps, 0))
    const = lambda a: _resident(a.shape, lambda i: (0,) * a.ndim)
    in_specs = [row(D_MODEL), const(g), const(w1), const(gq), const(gkv), const(wuq), const(wukv),
                tab(256), tab(256), tab(256), tab(256), tab(256), tab(256), tab(512), tab(512), tab(512)]
    kb = tm // MOBA_BLOCK
    out_shapes = []
    out_specs = []

    def add(width, dtype, transposed):
        if transposed:
            out_shapes.append(jax.ShapeDtypeStruct((width, n), dtype))
            out_specs.append(pl.BlockSpec((width, tm), lambda i: (0, i)))
        else:
            out_shapes.append(jax.ShapeDtypeStruct((n, width), dtype))
            out_specs.append(row(width))

    add(256, BF16, True); add(256, BF16, False); add(512, BF16, True)
    add(256, BF16, True); add(256, BF16, False); add(IDX_HEADS, F32, True)
    add(256, BF16, True); add(256, BF16, False); add(512, BF16, True)
    add(256, BF16, True); add(256, BF16, False); add(512, BF16, True)
    out_shapes.append(jax.ShapeDtypeStruct((nt, kb, QKV_W), F32))
    out_specs.append(pl.BlockSpec((1, kb, QKV_W), lambda i: (i, 0, 0)))
    add(512, BF16, True); add(512, BF16, False); add(512, BF16, True)
    return pl.pallas_call(
        _proj_kernel,
        grid=(nt,),
        in_specs=in_specs,
        out_specs=out_specs,
        out_shape=out_shapes,
        compiler_params=_params("arbitrary"),
        name="proj",
    )(x2, g, w1, gq, gkv, wuq, wukv, *tables)


def _stack_heads_t(q_t, n_heads, head_dim):
    q32 = q_t.astype(F32)
    sub_head = lax.broadcasted_iota(I32, q32.shape, 0) // head_dim
    return jnp.concatenate([jnp.where(sub_head == h, q32, 0.0) for h in range(n_heads)], axis=1).astype(BF16)


def _kblock(ref, j, t):
    return ref[pl.ds(pl.multiple_of(j * t, t), t), :]


def _vblock_t(v_t_ref, h, j, t):
    return v_t_ref[h * HEAD_SLOT:(h + 1) * HEAD_SLOT, pl.ds(pl.multiple_of(j * t, t), t)]


def _softmax_tile(s, v_t_ref, j, m_ref, acc_refs, t):
    m_old = m_ref[...]
    m_new = jnp.maximum(m_old, jnp.max(s, axis=0, keepdims=True))
    alpha = jnp.exp2(m_old - m_new)
    p = jnp.exp2(s - m_new).astype(BF16)
    for h, acc_ref in enumerate(acc_refs):
        cols = slice(h * t, (h + 1) * t)
        acc_ref[...] = alpha[:, cols] * acc_ref[...] + _dot(_vblock_t(v_t_ref, h, j, t), p[:, cols])
    m_ref[...] = m_new


def _softmax_init(m_ref, acc_refs):
    m_ref[...] = jnp.full(m_ref.shape, NEG, F32)
    for acc_ref in acc_refs:
        acc_ref[...] = jnp.zeros(acc_ref.shape, F32)


def _softmax_finish(o_ref, acc_refs):
    for h, acc_ref in enumerate(acc_refs):
        a = acc_ref[...]
        o = a / a[ONES_LANE:ONES_LANE + 1, :]
        o_ref[:, h * HEAD_SLOT:(h + 1) * HEAD_SLOT] = o.T.astype(BF16)


def _wide_causal(t, strict):
    keys = lax.broadcasted_iota(I32, (t, N_HEADS * t), 0)
    qrys = lax.broadcasted_iota(I32, (t, N_HEADS * t), 1) % t
    return keys < qrys if strict else keys <= qrys


def _tile_ids(t):
    return lax.broadcasted_iota(I32, (t, t), 0), lax.broadcasted_iota(I32, (t, t), 1)


def _sweep(n, logits, consume, s_refs):
    s_refs[0][...] = logits(0)

    def pair(pi, carry):
        p = 2 * pi
        s_refs[1][...] = logits(p + 1)
        consume(s_refs[0], p, False)
        s_refs[0][...] = logits(p + 2)
        consume(s_refs[1], p + 1, False)
        return carry

    lax.fori_loop(0, (n - 1) // 2, pair, 0)

    @pl.when(n % 2 == 1)
    def _():
        consume(s_refs[0], n - 1, True)

    @pl.when(n % 2 == 0)
    def _():
        s_refs[1][...] = logits(n - 1)
        consume(s_refs[0], n - 2, False)
        consume(s_refs[1], n - 1, True)


def _att_specs(t, seq, q_heights, k_widths, n_vt):
    nq = seq // t
    qs = [pl.BlockSpec((w, t), lambda b, i: (0, b * nq + i)) for w in q_heights]
    ks = [_resident((seq, w), lambda b, i: (b, 0)) for w in k_widths]
    vs = [_resident((VPAD_W, seq), lambda b, i: (0, b)) for _ in range(n_vt)]
    return qs, ks, vs


def _att_out(n, seq, t):
    nq = seq // t
    return (jax.ShapeDtypeStruct((n, VPAD_W), BF16),
            pl.BlockSpec((t, VPAD_W), lambda b, i: (b * nq + i, 0)))


def _att_scratch(t):
    return ([pltpu.VMEM((t, N_HEADS * t), F32)] * 2 + [pltpu.VMEM((1, N_HEADS * t), F32)]
            + [pltpu.VMEM((HEAD_SLOT, t), F32)] * N_HEADS)


def _split_scratch(scratch):
    return scratch[0:2], scratch[2], scratch[3:3 + N_HEADS]


def _order_key(x):
    bits = pltpu.bitcast(x + 0.0, I32)
    return bits ^ ((bits >> 31) & 0x7FFFFFFF)


def _dsa_kernel(qa_ref, qi_ref, iw_ref, ka_ref, ki_ref, va_ref, o_ref, keys_ref, hi_ref, lo_ref, r0_ref, r1_ref,
                *scratch, t, topk):
    s_refs, m_ref, acc_refs = _split_scratch(scratch)
    i = pl.program_id(1)
    nblk = i + 1
    key_ids, qry_ids = _tile_ids(t)

    qis = _stack_heads_t(qi_ref[...], IDX_HEADS, IDX_DIM)
    iw = iw_ref[...]

    def score_tile(r_ref, j, last):
        sc = iw[0:1, :] * jnp.maximum(r_ref[:, 0:t], 0.0)
        for h in range(1, IDX_HEADS):
            sc = sc + iw[h:h + 1, :] * jnp.maximum(r_ref[:, h * t:(h + 1) * t], 0.0)
        key = _order_key(sc)
        if last:
            key = jnp.where(key_ids <= qry_ids, key, INT_MIN)
        sl = pl.ds(pl.multiple_of(j * t, t), t)
        keys_ref[sl, :] = key
        hi_ref[sl, :] = (key >> 16).astype(jnp.int16)
        lo_ref[sl, :] = ((key & 0xFFFF) - 32768).astype(jnp.int16)

    _sweep(nblk, lambda p: _dot(_kblock(ki_ref, p, t), qis), score_tile, (r0_ref, r1_ref))

    def count16(ref, cand):
        cand16 = cand.astype(jnp.int16)

        def tile_hits(c):
            tile = ref[pl.ds(pl.multiple_of(c * t, t), t), :]
            hit = jnp.where(tile >= cand16, jnp.int16(1), jnp.int16(0))
            return [hit[g * 16:(g + 1) * 16] for g in range(t // 16)]

        def tree_sum(parts):
            while len(parts) > 1:
                parts = [parts[a] + parts[a + 1] for a in range(0, len(parts), 2)]
            return parts[0]

        def pair(c, acc):
            return acc + tree_sum(tile_hits(2 * c) + tile_hits(2 * c + 1))

        acc = lax.fori_loop(0, nblk // 2, pair, jnp.zeros((16, t), jnp.int16))
        acc = lax.cond(nblk % 2 == 1, lambda a: a + tree_sum(tile_hits(nblk - 1)), lambda a: a, acc)
        return jnp.sum(acc.astype(I32), axis=0, keepdims=True)

    def kth_largest16(ref, kth):
        c0 = count16(ref, jnp.zeros((1, t), I32))
        v0 = jnp.where(c0 >= kth, 0, -32768).astype(I32)

        def bit_step(b, v):
            cand = v | lax.shift_left(jnp.int32(1), 14 - b)
            return jnp.where(count16(ref, cand) >= kth, cand, v)
        return lax.fori_loop(0, 15, bit_step, v0)

    tau_hi = kth_largest16(hi_ref, jnp.full((1, t), topk, I32))
    above = jnp.where(tau_hi == 32767, 0, count16(hi_ref, jnp.minimum(tau_hi + 1, 32767)))
    tau_hi16 = tau_hi.astype(jnp.int16)

    def keep_low(c, carry):
        sl = pl.ds(pl.multiple_of(c * t, t), t)
        lo_ref[sl, :] = jnp.where(hi_ref[sl, :] == tau_hi16, lo_ref[sl, :], jnp.int16(-32768))
        return carry

    lax.fori_loop(0, nblk, keep_low, 0)
    tau_lo = kth_largest16(lo_ref, topk - above)
    tau = lax.shift_left(tau_hi, 16) | (tau_lo + 32768)
    tau = jnp.maximum(tau, INT_MIN + 1)

    def count(pred):
        def body(c, acc):
            hit = pred(keys_ref[pl.ds(pl.multiple_of(c * t, t), t), :]).astype(I32)
            part = hit[0:8]
            for g in range(1, t // 8):
                part = part + hit[g * 8:(g + 1) * 8]
            return acc + part
        acc = lax.fori_loop(0, nblk, body, jnp.zeros((8, t), I32))
        return jnp.sum(acc, axis=0, keepdims=True)

    c_ge = count(lambda k: k >= tau)

    @pl.when(jnp.max(c_ge) > topk)
    def _():
        c_gt = count(lambda k: k > tau)
        allowed = (topk - c_gt).astype(F32)
        prefix_incl = (qry_ids <= key_ids).astype(BF16)

        def fix(c, seen):
            sl = pl.ds(pl.multiple_of(c * t, t), t)
            k = keys_ref[sl, :]
            eq = k == tau
            rank = seen + _dot(prefix_incl, eq.astype(BF16))
            keys_ref[sl, :] = jnp.where(eq & (rank > allowed), tau - 1, k)
            return seen + jnp.sum(eq.astype(F32), axis=0, keepdims=True)

        lax.fori_loop(0, nblk, fix, jnp.zeros((1, t), F32))

    qs = _stack_heads_t(qa_ref[...], N_HEADS, HEAD_DIM)
    _softmax_init(m_ref, acc_refs)

    def consume(s_ref, p, last):
        sel = keys_ref[pl.ds(pl.multiple_of(p * t, t), t), :] >= tau
        s = jnp.concatenate([jnp.where(sel, s_ref[:, h * t:(h + 1) * t], NEG) for h in range(N_HEADS)], axis=1)
        _softmax_tile(s, va_ref, p, m_ref, acc_refs, t)

    _sweep(nblk, lambda p: _dot(_kblock(ka_ref, p, t), qs), consume, s_refs)
    _softmax_finish(o_ref, acc_refs)


def _dsa_call(qa_t, qi_t, iw_t, ka, ki, va_t, bsz, seq):
    t = ATT_T
    topk = min(DSA_TOPK_MAX, seq // 4)
    qs, ks, vs = _att_specs(t, seq, (QKV_W, QKV_W, IDX_HEADS), (QKV_W, QKV_W), 1)
    out_shape, out_spec = _att_out(bsz * seq, seq, t)
    return pl.pallas_call(
        functools.partial(_dsa_kernel, t=t, topk=topk),
        grid=(bsz, seq // t),
        in_specs=qs + ks + vs,
        out_specs=out_spec,
        out_shape=out_shape,
        scratch_shapes=[pltpu.VMEM((seq, t), I32), pltpu.VMEM((seq, t), jnp.int16), pltpu.VMEM((seq, t), jnp.int16)]
        + [pltpu.VMEM((t, IDX_HEADS * t), F32)] * 2 + _att_scratch(t),
        compiler_params=_params("arbitrary", "arbitrary"),
        name="dsa",
    )(qa_t, qi_t, iw_t, ka, ki, va_t)


def _sb_kernel(q_ref, k_ref, v_ref, o_ref, *scratch, t):
    s_refs, carry_ref, acc_refs = _split_scratch(scratch)
    i = pl.program_id(1)
    qs = _stack_heads_t(q_ref[...], N_HEADS, HEAD_DIM)
    key_ids, qry_ids = _tile_ids(t)
    suffix = (qry_ids > key_ids).astype(BF16)
    carry_ref[...] = jnp.zeros(carry_ref.shape, F32)
    for acc_ref in acc_refs:
        acc_ref[...] = jnp.zeros(acc_ref.shape, F32)

    def tile_step(z, j, past):
        neg_abs = pltpu.bitcast(pltpu.bitcast(z, I32) | INT_MIN, F32)
        nlf = jnp.maximum(z, 0.0) + jnp.log2(1.0 + jnp.exp2(neg_abs))
        if past is not None:
            nlf = jnp.where(past, nlf, 0.0)
        later = _dot(suffix, nlf.astype(BF16))
        a = jnp.exp2(z - (nlf + (later + carry_ref[...])))
        if past is not None:
            a = jnp.where(past, a, 0.0)
        a = a.astype(BF16)
        for h in range(N_HEADS):
            acc_refs[h][...] += _dot(_vblock_t(v_ref, h, j, t), a[:, h * t:(h + 1) * t])
        carry_ref[...] += jnp.sum(nlf, axis=0, keepdims=True)

    tile_step(_dot(_kblock(k_ref, i, t), qs), i, _wide_causal(t, strict=True))

    def consume(s_ref, p, last):
        tile_step(s_ref[...], i - 1 - p, None)

    @pl.when(i > 0)
    def _():
        _sweep(i, lambda p: _dot(_kblock(k_ref, i - 1 - p, t), qs), consume, s_refs)

    for h in range(N_HEADS):
        o_ref[:, h * HEAD_SLOT:(h + 1) * HEAD_SLOT] = acc_refs[h][...].T.astype(BF16)


def _sb_call(q_t, k, v_t, bsz, seq):
    t = ATT_T
    qs, ks, vs = _att_specs(t, seq, (QKV_W,), (QKV_W,), 1)
    out_shape, out_spec = _att_out(bsz * seq, seq, t)
    return pl.pallas_call(
        functools.partial(_sb_kernel, t=t),
        grid=(bsz, seq // t),
        in_specs=qs + ks + vs,
        out_specs=out_spec,
        out_shape=out_shape,
        scratch_shapes=_att_scratch(t),
        compiler_params=_params("arbitrary", "arbitrary"),
        name="stickbreak",
    )(q_t, k, v_t)


def _moba_kernel(q_ref, kmean_ref, k_ref, v_ref, o_ref, chosen_ref, *scratch, t, n_sel):
    s_refs, m_ref, acc_refs = _split_scratch(scratch)
    i = pl.program_id(1)
    qs = _stack_heads_t(q_ref[...], N_HEADS, HEAD_DIM)
    cols_all = N_HEADS * t
    gate = _dot(kmean_ref[0], qs)
    blk = lax.broadcasted_iota(I32, (LANES, cols_all), 0)
    ninf = -jnp.inf
    g = jnp.where(blk < i, gate, ninf)
    chosen = jnp.zeros((LANES, cols_all), F32)
    for _ in range(n_sel):
        mx = jnp.max(g, axis=0, keepdims=True)
        first = jnp.min(jnp.where(g == mx, blk, LANES), axis=0, keepdims=True)
        pick = (blk == first) & (mx > ninf)
        chosen = jnp.where(pick, 1.0, chosen)
        g = jnp.where(pick, ninf, g)
    chosen_ref[...] = chosen
    _softmax_init(m_ref, acc_refs)

    def consume(s_ref, p, last):
        if last:
            keep = _wide_causal(t, strict=False)
        else:
            keep = chosen_ref[pl.ds(p, 1), :] > 0.0
        _softmax_tile(jnp.where(keep, s_ref[...], NEG), v_ref, p, m_ref, acc_refs, t)

    _sweep(i + 1, lambda p: _dot(_kblock(k_ref, p, t), qs), consume, s_refs)
    _softmax_finish(o_ref, acc_refs)


def _moba_call(q_t, kmean, k, v_t, bsz, seq):
    t = ATT_T
    assert t == MOBA_BLOCK
    n_sel = min(MOBA_TOPK, seq // MOBA_BLOCK)
    qs, ks, vs = _att_specs(t, seq, (QKV_W,), (QKV_W,), 1)
    out_shape, out_spec = _att_out(bsz * seq, seq, t)
    return pl.pallas_call(
        functools.partial(_moba_kernel, t=t, n_sel=n_sel),
        grid=(bsz, seq // t),
        in_specs=qs + [_resident((1, LANES, QKV_W), lambda b, i: (b, 0, 0))] + ks + vs,
        out_specs=out_spec,
        out_shape=out_shape,
        scratch_shapes=[pltpu.VMEM((LANES, N_HEADS * t), F32)] + _att_scratch(t),
        compiler_params=_params("arbitrary", "arbitrary"),
        name="moba",
    )(q_t, kmean, k, v_t)


def _mla_kernel(q_ref, k_ref, v_ref, o_ref, *scratch, t):
    s_refs, m_ref, acc_refs = _split_scratch(scratch)
    i = pl.program_id(1)
    q_t = q_ref[...]
    _softmax_init(m_ref, acc_refs)

    def logits(p):
        kblk = _kblock(k_ref, p, t)
        return jnp.concatenate(
            [_dot(kblk[:, h * HEAD_SLOT:(h + 1) * HEAD_SLOT], q_t[h * HEAD_SLOT:(h + 1) * HEAD_SLOT, :])
             for h in range(N_HEADS)], axis=1)

    def consume(s_ref, p, last):
        s = s_ref[...]
        if last:
            s = jnp.where(_wide_causal(t, strict=False), s, NEG)
        _softmax_tile(s, v_ref, p, m_ref, acc_refs, t)

    _sweep(i + 1, logits, consume, s_refs)
    _softmax_finish(o_ref, acc_refs)


def _mla_call(q_t, k, v_t, bsz, seq):
    t = ATT_T
    qs, ks, vs = _att_specs(t, seq, (VPAD_W,), (VPAD_W,), 1)
    out_shape, out_spec = _att_out(bsz * seq, seq, t)
    return pl.pallas_call(
        functools.partial(_mla_kernel, t=t),
        grid=(bsz, seq // t),
        in_specs=qs + ks + vs,
        out_specs=out_spec,
        out_shape=out_shape,
        scratch_shapes=_att_scratch(t),
        compiler_params=_params("arbitrary", "arbitrary"),
        name="mla",
    )(q_t, k, v_t)


def _tail_kernel(x_ref, g_ref, wg_ref, bg_ref, oa_ref, ob_ref, oc_ref, od_ref, wb_ref, wout_ref,
                 g2_ref, wup_ref, wdown_ref, gf_ref, out_ref, *, final_norm):
    x = x_ref[...]
    hb = _rms(x, g_ref[...]).astype(BF16)
    merged = None
    for n, o_ref in enumerate((oa_ref, ob_ref, oc_ref, od_ref)):
        logit = _dot(hb, wg_ref[:, n * D_MODEL:(n + 1) * D_MODEL]) + bg_ref[n:n + 1, :]
        gate = 1.0 / (1.0 + jnp.exp(-logit))
        term = gate * _dot(o_ref[...], wb_ref[n])
        merged = term if merged is None else merged + term
    x1 = x + _dot(merged.astype(BF16), wout_ref[...])
    hb2 = _rms(x1, g2_ref[...]).astype(BF16)
    y = x1
    for c in range(D_FF // D_MODEL):
        u = jnp.maximum(_dot(hb2, wup_ref[:, c * D_MODEL:(c + 1) * D_MODEL]), 0.0)
        y = y + _dot((u * u).astype(BF16), wdown_ref[c * D_MODEL:(c + 1) * D_MODEL, :])
    if final_norm:
        y = _rms(y, gf_ref[...])
    out_ref[...] = y


def _tail_call(x2, g, wg, bg, oa, ob, oc, od, wb, wout, g2, wup, wdown, gf, final_norm):
    n = x2.shape[0]
    tm = PROJ_ROWS
    row = lambda w: pl.BlockSpec((tm, w), lambda i: (i, 0))
    const = lambda a: _resident(a.shape, lambda i: (0,) * a.ndim)
    return pl.pallas_call(
        functools.partial(_tail_kernel, final_norm=final_norm),
        grid=(n // tm,),
        in_specs=[row(D_MODEL), const(g), const(wg), const(bg), row(VPAD_W), row(VPAD_W), row(VPAD_W),
                  row(VPAD_W), const(wb), const(wout), const(g2), const(wup), const(wdown), const(gf)],
        out_specs=row(D_MODEL),
        out_shape=jax.ShapeDtypeStruct((n, D_MODEL), F32),
        compiler_params=_params("arbitrary"),
        name="tail",
    )(x2, g, wg, bg, oa, ob, oc, od, wb, wout, g2, wup, wdown, gf)


def _pad_last(w, n_groups, width, slot):
    lead = w.shape[:-1]
    w = w.reshape(*lead, n_groups, width)
    w = jnp.pad(w, [(0, 0)] * (len(lead) + 1) + [(0, slot - width)])
    return w.reshape(*lead, n_groups * slot)


def _pack_in_proj(w_in):
    offs = [0]
    for w in IN_WIDTHS:
        offs.append(offs[-1] + w)
    (dq, dk, dv, iq, ik, iw, sq, sk, sv, mq, mk, mv, cq, ckv, kr, gate) = [
        w_in[:, offs[n]:offs[n + 1]] for n in range(len(IN_WIDTHS))]
    vpad = lambda w: _pad_last(w, N_HEADS, HEAD_DIM, HEAD_SLOT)
    d = w_in.shape[0]
    kr_slot = jnp.concatenate([jnp.zeros((d, MLA_NOPE), F32), kr, jnp.zeros((d, HEAD_SLOT - MLA_NOPE - MLA_ROPE), F32)], axis=1)
    w1 = jnp.concatenate([
        dq, dk, vpad(dv),
        iq, jnp.tile(ik, (1, IDX_HEADS)), _pad_last(iw, 1, IDX_HEADS, LANES),
        sq, sk, vpad(sv),
        mq, mk, vpad(mv),
        cq, ckv, jnp.tile(kr_slot, (1, N_HEADS)),
    ], axis=1)
    assert w1.shape[1] == _G_END
    return w1.astype(BF16), gate.astype(BF16)


def _rope_tables(n_rot, seq):
    inv_freq = 1.0 / (ROPE_THETA ** (jnp.arange(0, n_rot, 2, dtype=F32) / n_rot))
    ang = jnp.arange(seq, dtype=F32)[:, None] * inv_freq[None, :]
    return jnp.cos(ang), jnp.sin(ang)


def _rope_lane_tables(n_rot, seq, lead, period, reps):
    cos, sin = _rope_tables(n_rot, seq)
    nh = n_rot // 2
    tail = period - lead - n_rot
    one = lambda n: jnp.ones((seq, n), F32)
    zero = lambda n: jnp.zeros((seq, n), F32)
    c = jnp.concatenate([one(lead), cos, cos, one(tail)], axis=1)
    s1 = jnp.concatenate([zero(lead), -sin, zero(nh), zero(tail)], axis=1)
    s2 = jnp.concatenate([zero(lead), zero(nh), sin, zero(tail)], axis=1)
    return tuple(jnp.tile(a, (1, reps)) for a in (c, s1, s2))


def kernel(x, norm_mix, w_in, b_gate, mla_q_norm, mla_kv_norm, w_uq, w_ukv, w_branch, w_out, norm_mlp, w_up,
           w_down, norm_final):
    bsz, seq, d = x.shape
    depth = w_in.shape[0]
    assert d == D_MODEL and seq % PROJ_ROWS == 0 and seq % ATT_T == 0 and seq // MOBA_BLOCK <= LANES
    n = bsz * seq
    tables = (_rope_lane_tables(PARTIAL_ROT, seq, 0, HEAD_DIM, N_HEADS)
              + _rope_lane_tables(IDX_ROT, seq, 0, IDX_DIM, IDX_HEADS)
              + _rope_lane_tables(MLA_ROPE, seq, MLA_NOPE, HEAD_SLOT, N_HEADS))
    row = lambda v: v.reshape(1, -1).astype(F32)
    nb = seq // MOBA_BLOCK
    x2 = x.reshape(n, d)
    for l in range(depth):
        w1, wg = _pack_in_proj(w_in[l])
        wuq = _pad_last(w_uq[l], N_HEADS, MLA_NOPE + MLA_ROPE, HEAD_SLOT).astype(BF16)
        ukv = w_ukv[l].reshape(MLA_KV_RANK, N_HEADS, 2 * HEAD_DIM)
        wukv = jnp.concatenate([
            _pad_last(ukv[:, :, :MLA_NOPE].reshape(MLA_KV_RANK, -1), N_HEADS, HEAD_DIM, HEAD_SLOT),
            _pad_last(ukv[:, :, MLA_NOPE:].reshape(MLA_KV_RANK, -1), N_HEADS, HEAD_DIM, HEAD_SLOT)], axis=1).astype(BF16)
        wb = jnp.pad(w_branch[l].reshape(N_BRANCH, N_HEADS, HEAD_DIM, D_MODEL),
                     ((0, 0), (0, 0), (0, HEAD_SLOT - HEAD_DIM), (0, 0))).reshape(N_BRANCH, VPAD_W, D_MODEL).astype(BF16)

        (qa, ka, va, qi, ki, iw, qb, kb, vb, qc, kc, vc, kmean, qd, kd, vd) = _proj_call(
            x2, row(norm_mix[l]), w1, row(mla_q_norm[l]), row(mla_kv_norm[l]), wuq, wukv, tables, seq)
        kmean = jnp.pad(kmean.reshape(bsz, nb, QKV_W), ((0, 0), (0, LANES - nb), (0, 0))).astype(BF16)

        oa = _dsa_call(qa, qi, iw, ka, ki, va, bsz, seq)
        ob = _sb_call(qb, kb, vb, bsz, seq)
        oc = _moba_call(qc, kmean, kc, vc, bsz, seq)
        od = _mla_call(qd, kd, vd, bsz, seq)

        x2 = _tail_call(x2, row(norm_mix[l]), wg, b_gate[l].astype(F32), oa, ob, oc, od, wb, w_out[l].astype(BF16),
                        row(norm_mlp[l]), w_up[l].astype(BF16), w_down[l].astype(BF16), row(norm_final),
                        final_norm=(l == depth - 1))
    return x2.reshape(bsz, seq, d)
```
